```python
import jax, jax.numpy as jnp
from jax import lax
import numpy as np

D_MODEL = 2048
BATCH = 8
SEQ = 2048
DEPTH = 1

CONV_WIDTH = D_MODEL // 2
CONV_K = 3
MLSTM_WIDTH = D_MODEL - CONV_WIDTH
MLSTM_HEADS = 4
MLSTM_V = MLSTM_WIDTH // MLSTM_HEADS
MLSTM_QK = MLSTM_V // 2
MLSTM_CHUNK = 64
N_GATE_COLS = 2 * 2 * MLSTM_HEADS
IN_SIZES = [CONV_WIDTH, CONV_WIDTH, CONV_WIDTH,
            MLSTM_HEADS * MLSTM_QK, MLSTM_HEADS * MLSTM_QK,
            MLSTM_WIDTH, MLSTM_WIDTH, N_GATE_COLS]
IN_COLS = sum(IN_SIZES)
PEER_HEADS = 8
N_KEYS = 128
N_EXPERTS = N_KEYS * N_KEYS
PEER_TOPK = 16
PEER_KEY_DIM = 256
PEER_BLOCK = 128
LN_EPS = 1e-5
ALPHA = (2 * DEPTH) ** 0.25
BETA = (8 * DEPTH) ** -0.25

kernel_name = 'hybrid_conv_mlstm_peer_deepnorm_adaln'


def _ln_stats(x):
    xf = x.astype(jnp.float32)
    mu = jnp.mean(xf, -1, keepdims=True)
    var = jnp.mean(jnp.square(xf - mu), -1, keepdims=True)
    return (xf - mu) * lax.rsqrt(var + LN_EPS)


def _modulate(x, shift, scale):
    return (_ln_stats(x) * (1.0 + scale[:, None, :].astype(jnp.float32))
            + shift[:, None, :].astype(jnp.float32)).astype(x.dtype)


def _post_norm(z, w, b):
    return (_ln_stats(z) * w.astype(jnp.float32) + b.astype(jnp.float32)).astype(z.dtype)


def _short_conv(z, w, b):
    ch = z.shape[-1]
    y = lax.conv_general_dilated(z, w[:, None, :].astype(z.dtype), window_strides=(1,),
                                 padding=[(CONV_K // 2, CONV_K // 2)],
                                 dimension_numbers=('NWC', 'WIO', 'NWC'),
                                 feature_group_count=ch)
    return y + b


def _mlstm_chunk_scan(q, k, v, log_f, log_i):
    G, S, H, dk = q.shape
    dv = v.shape[-1]
    L = MLSTM_CHUNK
    nc = S // L

    def chunks(t):
        t = t.reshape((G, nc, L, H) + t.shape[3:])
        return jnp.moveaxis(t, (1, 3), (0, 2))

    tril = jnp.tril(jnp.ones((L, L), dtype=bool))

    def body(carry, xs):
        C, n, m = carry
        qc, kc, vc, lf, ig = xs
        b = jnp.cumsum(lf, axis=-1)
        d = b[..., :, None] - b[..., None, :] + ig[..., None, :]
        d = jnp.where(tril, d, -jnp.inf)
        inter = b + m[..., None]
        m_t = jnp.maximum(inter, jnp.max(d, axis=-1))
        w_intra = jnp.exp(d - m_t[..., None])
        w_inter = jnp.exp(inter - m_t)
        s = jnp.einsum('ghtd,ghsd->ghts', qc, kc) * w_intra
        num = (jnp.einsum('ghts,ghsv->ghtv', s, vc)
               + w_inter[..., None] * jnp.einsum('ghtd,ghdv->ghtv', qc, C))
        den = jnp.sum(s, -1) + w_inter * jnp.einsum('ghtd,ghd->ght', qc, n)
        h = num / jnp.maximum(jnp.abs(den), jnp.exp(-m_t))[..., None]
        b_last = b[..., -1]
        a = b_last[..., None] - b + ig
        m_new = jnp.maximum(b_last + m, jnp.max(a, axis=-1))
        w_a = jnp.exp(a - m_new[..., None])
        decay = jnp.exp(b_last + m - m_new)
        C_new = decay[..., None, None] * C + jnp.einsum('ghs,ghsd,ghsv->ghdv', w_a, kc, vc)
        n_new = decay[..., None] * n + jnp.einsum('ghs,ghsd->ghd', w_a, kc)
        return (C_new, n_new, m_new), h

    init = (jnp.zeros((G, H, dk, dv), jnp.float32), jnp.zeros((G, H, dk), jnp.float32),
            jnp.zeros((G, H), jnp.float32))
    xs = (chunks(q), chunks(k), chunks(v), chunks(log_f), chunks(log_i))
    _, ys = lax.scan(body, init, xs)
    return jnp.moveaxis(ys, (0, 2), (1, 3)).reshape(G, S, H, dv)


def _mlstm_bidir(q, k, v, gates):
    B = q.shape[0]
    flip = lambda t: jnp.flip(t, axis=1)
    both = lambda t: jnp.concatenate([t, flip(t)], axis=0)
    log_i = jnp.concatenate([gates[:, :, 0, 0], flip(gates[:, :, 1, 0])], axis=0)
    log_f = jax.nn.log_sigmoid(
        jnp.concatenate([gates[:, :, 0, 1], flip(gates[:, :, 1, 1])], axis=0))
    h = _mlstm_chunk_scan(both(q), both(k), both(v), log_f, log_i)
    return h[:B] + flip(h[B:])


def _hybrid_mixer(u, w_in, b_in, conv_w, conv_b, norm_w, w_out):
    B, S, _ = u.shape
    proj = u @ w_in + b_in
    offsets = [int(o) for o in np.cumsum(IN_SIZES)[:-1]]
    gb, gc, hin, q, k, v, o, gates = jnp.split(proj, offsets, axis=-1)
    y_conv = gb * _short_conv(gc * hin, conv_w, conv_b)
    f32 = jnp.float32
    q = q.reshape(B, S, MLSTM_HEADS, MLSTM_QK).astype(f32) * (MLSTM_QK ** -0.5)
    k = k.reshape(B, S, MLSTM_HEADS, MLSTM_QK).astype(f32)
    v = v.reshape(B, S, MLSTM_HEADS, MLSTM_V).astype(f32)
    gates = gates.reshape(B, S, 2, 2, MLSTM_HEADS).astype(f32)
    h = _mlstm_bidir(q, k, v, gates)
    mu = jnp.mean(h, -1, keepdims=True)
    var = jnp.mean(jnp.square(h - mu), -1, keepdims=True)
    hn = (h - mu) * lax.rsqrt(var + LN_EPS) * norm_w.reshape(MLSTM_HEADS, MLSTM_V).astype(f32)
    og = jax.nn.sigmoid(o.astype(f32)).reshape(B, S, MLSTM_HEADS, MLSTM_V)
    y_m = (og * hn).reshape(B, S, MLSTM_WIDTH).astype(u.dtype)
    return jnp.concatenate([y_conv, y_m], axis=-1) @ w_out


def _peer(u, wq, subkeys, pu, pv):
    B, S, D = u.shape
    T = B * S
    ut = u.reshape(T, D)
    qh = (ut @ wq).reshape(T, PEER_HEADS, 2, PEER_KEY_DIM // 2)
    scores = jnp.einsum('thpd,hpnd->thpn', qh, subkeys)
    top_s, top_i = lax.top_k(scores, PEER_TOPK)
    cand = top_s[:, :, 0, :, None] + top_s[:, :, 1, None, :]
    best_s, best_c = lax.top_k(cand.reshape(T, PEER_HEADS, PEER_TOPK * PEER_TOPK), PEER_TOPK)
    i1 = jnp.take_along_axis(top_i[:, :, 0], best_c // PEER_TOPK, axis=-1)
    i2 = jnp.take_along_axis(top_i[:, :, 1], best_c % PEER_TOPK, axis=-1)
    expert = i1 * N_KEYS + i2
    gate = jax.nn.softmax(best_s.astype(jnp.float32), axis=-1).astype(u.dtype)
    nb = T // PEER_BLOCK

    def block(args):
        xb, eb, gbk = args
        act = jax.nn.gelu(jnp.einsum('pd,phkd->phk', xb, pu[eb]), approximate=False)
        return jnp.einsum('phk,phkd->pd', gbk * act, pv[eb])

    out = lax.map(block, (ut.reshape(nb, PEER_BLOCK, D),
                          expert.reshape(nb, PEER_BLOCK, PEER_HEADS, PEER_TOPK),
                          gate.reshape(nb, PEER_BLOCK, PEER_HEADS, PEER_TOPK)))
    return out.reshape(B, S, D)


def setup_inputs(seed: int = 0) -> dict:
    key = jax.random.key(seed)
    ks = jax.random.split(key, 20)
    D = D_MODEL
    nrm = lambda k, shape, std: jax.random.normal(k, shape, jnp.float32) * std
    f_bias = jnp.linspace(3.0, 6.0, MLSTM_HEADS)
    gate_offset = jnp.tile(jnp.stack([jnp.zeros(MLSTM_HEADS), f_bias])[None], (2, 1, 1)).reshape(N_GATE_COLS)
    b_in = nrm(ks[5], (DEPTH, IN_COLS), 0.02).at[:, -N_GATE_COLS:].add(gate_offset)
    return {
        'x': nrm(ks[0], (BATCH, SEQ, D), 1.0),
        'c': nrm(ks[1], (BATCH, D), 1.0),
        'w_ada': nrm(ks[2], (DEPTH, D, 6 * D), D ** -0.5),
        'b_ada': nrm(ks[3], (DEPTH, 6 * D), 0.02),
        'w_in': nrm(ks[4], (DEPTH, D, IN_COLS), D ** -0.5),
        'b_in': b_in,
        'conv_w': nrm(ks[6], (DEPTH, CONV_K, CONV_WIDTH), CONV_K ** -0.5),
        'conv_b': nrm(ks[7], (DEPTH, CONV_WIDTH), 0.02),
        'mlstm_norm_w': 1.0 + nrm(ks[8], (DEPTH, MLSTM_WIDTH), 0.02),
        'w_out': nrm(ks[9], (DEPTH, D, D), BETA * D ** -0.5),
        'ln1_w': 1.0 + nrm(ks[10], (DEPTH, D), 0.02),
        'ln1_b': nrm(ks[11], (DEPTH, D), 0.02),
        'peer_wq': nrm(ks[12], (DEPTH, D, PEER_HEADS * PEER_KEY_DIM), D ** -0.5),
        'peer_subkeys': nrm(ks[13], (DEPTH, PEER_HEADS, 2, N_KEYS, PEER_KEY_DIM // 2), (PEER_KEY_DIM // 2) ** -0.5),
        'peer_u': nrm(ks[14], (DEPTH, N_EXPERTS, D), D ** -0.5),
        'peer_v': nrm(ks[15], (DEPTH, N_EXPERTS, D), BETA),
        'ln2_w': 1.0 + nrm(ks[16], (DEPTH, D), 0.02),
        'ln2_b': nrm(ks[17], (DEPTH, D), 0.02),
    }


def reference(x, c, w_ada, b_ada, w_in, b_in, conv_w, conv_b, mlstm_norm_w, w_out,
              ln1_w, ln1_b, peer_wq, peer_subkeys, peer_u, peer_v, ln2_w, ln2_b):
    cond = jax.nn.silu(c)
    for l in range(DEPTH):
        mod = cond @ w_ada[l] + b_ada[l]
        sh1, sc1, g1, sh2, sc2, g2 = jnp.split(mod, 6, axis=-1)
        u = _modulate(x, sh1, sc1)
        y = _hybrid_mixer(u, w_in[l], b_in[l], conv_w[l], conv_b[l], mlstm_norm_w[l], w_out[l])
        x = _post_norm(ALPHA * x + g1[:, None, :] * y, ln1_w[l], ln1_b[l])
        u = _modulate(x, sh2, sc2)
        y = _peer(u, peer_wq[l], peer_subkeys[l], peer_u[l], peer_v[l])
        x = _post_norm(ALPHA * x + g2[:, None, :] * y, ln2_w[l], ln2_b[l])
    return x
```

```python
import functools

import jax
import jax.numpy as jnp
from jax import lax
from jax.experimental import pallas as pl
from jax.experimental.pallas import tpu as pltpu

D = 2048
CONV_W = 1024
CONV_K = 3
N_HEADS = 4
DQK = 128
DV = 256
N_GATE = 16
MAIN_COLS = 6144
COL_GB, COL_GC, COL_HIN, COL_Q, COL_K, COL_V, COL_O = 0, 1024, 2048, 3072, 3584, 4096, 5120
P_HEADS = 8
N_KEYS = 128
N_EXPERTS = N_KEYS * N_KEYS
TOPK = 16
LN_EPS = 1e-5
DEPTH = 1
ALPHA = (2 * DEPTH) ** 0.25

LANE = 128
SUBLANE = 8
VMEM_LIMIT = 52 * 1024 * 1024

ADA_TN = 1024
INPROJ_TM, INPROJ_TN = 512, 1024
MLSTM_L = 128
MIX_TM = 256
PQ_TM = 256
SEL_TM = 128
PEER_TM, PEER_EB = 512, 512

F32 = jnp.float32
BF16 = jnp.bfloat16
NEG_INF = float("-inf")


def _cparams(sem):
    return pltpu.CompilerParams(dimension_semantics=sem, vmem_limit_bytes=VMEM_LIMIT)


def _ln(x):
    mu = jnp.mean(x, axis=-1, keepdims=True)
    xc = x - mu
    var = jnp.mean(xc * xc, axis=-1, keepdims=True)
    return xc * lax.rsqrt(var + LN_EPS)


def _bdot(a, b):
    return jnp.dot(a, b, preferred_element_type=F32)


def _ada_kernel(c_ref, w_ref, b_ref, o_ref):
    c = c_ref[...]
    cond = c * jax.nn.sigmoid(c)
    o_ref[...] = _bdot(cond.astype(BF16), w_ref[...].astype(BF16)) + b_ref[...]


def _ada(c, w, b):
    bsz = c.shape[0]
    n = w.shape[1]
    return pl.pallas_call(
        _ada_kernel,
        grid=(n // ADA_TN,),
        in_specs=[pl.BlockSpec((bsz, D), lambda j: (0, 0)),
                  pl.BlockSpec((D, ADA_TN), lambda j: (0, j)),
                  pl.BlockSpec((1, ADA_TN), lambda j: (0, j))],
        out_specs=pl.BlockSpec((bsz, ADA_TN), lambda j: (0, j)),
        out_shape=jax.ShapeDtypeStruct((bsz, n), F32),
        compiler_params=_cparams(("arbitrary",)),
        name="ada",
    )(c, w, b.reshape(1, n))


def _inproj_kernel(x_ref, sh_ref, sc_ref, w_ref, b_ref, wg_ref, bg_ref, proj_ref, gate_ref, u_scr):
    @pl.when(pl.program_id(2) == 0)
    def _():
        u = _ln(x_ref[...]) * (1.0 + sc_ref[...]) + sh_ref[...]
        ub = u.astype(BF16)
        u_scr[...] = ub
        gate_ref[...] = _bdot(ub, wg_ref[...]) + bg_ref[...]

    proj_ref[...] = _bdot(u_scr[...], w_ref[...]) + b_ref[...]


def _inproj(x, sh, sc, w_main, b_main, w_gate, b_gate):
    bsz, s, _ = x.shape
    tm, tn = min(INPROJ_TM, s), INPROJ_TN
    return pl.pallas_call(
        _inproj_kernel,
        grid=(bsz, s // tm, MAIN_COLS // tn),
        in_specs=[pl.BlockSpec((None, tm, D), lambda b, i, j: (b, i, 0)),
                  pl.BlockSpec((None, 1, D), lambda b, i, j: (b, 0, 0)),
                  pl.BlockSpec((None, 1, D), lambda b, i, j: (b, 0, 0)),
                  pl.BlockSpec((D, tn), lambda b, i, j: (0, j)),
                  pl.BlockSpec((1, tn), lambda b, i, j: (0, j)),
                  pl.BlockSpec((D, LANE), lambda b, i, j: (0, 0)),
                  pl.BlockSpec((1, LANE), lambda b, i, j: (0, 0))],
        out_specs=[pl.BlockSpec((None, tm, tn), lambda b, i, j: (b, i, j)),
                   pl.BlockSpec((None, tm, LANE), lambda b, i, j: (b, i, 0))],
        out_shape=[jax.ShapeDtypeStruct((bsz, s, MAIN_COLS), F32),
                   jax.ShapeDtypeStruct((bsz, s, LANE), F32)],
        scratch_shapes=[pltpu.VMEM((tm, D), BF16)],
        compiler_params=_cparams(("arbitrary", "arbitrary", "arbitrary")),
        name="inproj",
    )(x, sh, sc, w_main, b_main, w_gate, b_gate)


def _log_sigmoid(x):
    return jnp.minimum(x, 0.0) - jnp.log1p(jnp.exp(-jnp.abs(x)))


def _mlstm_direction(d, q_ref, k_ref, v_ref, gcol_ref, grow_ref, h_ref, c_scr, n_scr, m_scr):
    L = q_ref.shape[0]
    t_idx = lax.broadcasted_iota(jnp.int32, (L, L), 0)
    s_idx = lax.broadcasted_iota(jnp.int32, (L, L), 1)
    mask = (t_idx >= s_idx) if d == 0 else (t_idx <= s_idx)
    mask_f = mask.astype(F32)
    mask_tf = ((t_idx <= s_idx) if d == 0 else (t_idx >= s_idx)).astype(F32)
    gcol = gcol_ref[...]
    grow = grow_ref[...]
    hi = lax.Precision.HIGHEST
    bcols = jnp.dot(mask_f, _log_sigmoid(gcol), precision=hi, preferred_element_type=F32)
    brows = jnp.dot(_log_sigmoid(grow), mask_tf, precision=hi, preferred_element_type=F32)
    last = L - 1 if d == 0 else 0
    for h in range(N_HEADS):
        idx = d * N_HEADS + h
        ci, cf = d * 8 + h, d * 8 + 4 + h
        b_col = bcols[:, cf:cf + 1]
        b_row = brows[cf:cf + 1, :]
        ig_col = gcol[:, ci:ci + 1]
        ig_row = grow[ci:ci + 1, :]
        b_tot = brows[cf:cf + 1, last:last + 1]
        m_prev = m_scr[idx][:, 0:1]
        dm = jnp.where(mask, b_col - b_row + ig_row, NEG_INF)
        inter = b_col + m_prev
        m_t = jnp.maximum(inter, jnp.max(dm, axis=1, keepdims=True))
        w_intra = jnp.exp(dm - m_t)
        w_inter = jnp.exp(inter - m_t)
        q = q_ref[:, h * DQK:(h + 1) * DQK] * (DQK ** -0.5)
        k = k_ref[:, h * DQK:(h + 1) * DQK]
        v = v_ref[:, h * DV:(h + 1) * DV]
        qb, kb = q.astype(BF16), k.astype(BF16)
        qk = lax.dot_general(qb, kb, (((1,), (1,)), ((), ())), preferred_element_type=F32)
        sw = qk * w_intra
        c_prev = c_scr[idx]
        n_prev = n_scr[idx]
        num = _bdot(sw.astype(BF16), v.astype(BF16)) + w_inter * _bdot(qb, c_prev.astype(BF16))
        den = (jnp.sum(sw, axis=1, keepdims=True)
               + w_inter * jnp.sum(q * n_prev, axis=1, keepdims=True))
        h_ref[:, h * DV:(h + 1) * DV] = num / jnp.maximum(jnp.abs(den), jnp.exp(-m_t))
        a_col = b_tot - b_col + ig_col
        m_new = jnp.maximum(b_tot + m_prev, jnp.max(a_col, axis=0, keepdims=True))
        w_a = jnp.exp(a_col - m_new)
        decay = jnp.exp(b_tot + m_prev - m_new)
        kv = lax.dot_general(kb, (w_a * v).astype(BF16), (((0,), (0,)), ((), ())),
                             preferred_element_type=F32)
        c_scr[idx] = decay * c_prev + kv
        n_scr[idx] = decay * n_prev + jnp.sum(w_a * k, axis=0, keepdims=True)
        m_scr[idx] = jnp.broadcast_to(m_new, (1, LANE))


def _mlstm_kernel(qf, kf, vf, gcf, grf, qb, kb, vb, gcb, grb, hf_ref, hb_ref, c_scr, n_scr, m_scr):
    @pl.when(pl.program_id(1) == 0)
    def _():
        c_scr[...] = jnp.zeros_like(c_scr)
        n_scr[...] = jnp.zeros_like(n_scr)
        m_scr[...] = jnp.zeros_like(m_scr)

    _mlstm_direction(0, qf, kf, vf, gcf, grf, hf_ref, c_scr, n_scr, m_scr)
    _mlstm_direction(1, qb, kb, vb, gcb, grb, hb_ref, c_scr, n_scr, m_scr)


def _mlstm(proj, gates, gates_t):
    bsz, s, _ = proj.shape
    L = min(MLSTM_L, s)
    nc = s // L
    qkw, vw = N_HEADS * DQK, N_HEADS * DV

    def specs(row):
        return [pl.BlockSpec((None, L, qkw), lambda b, j: (b, row(j), COL_Q // qkw)),
                pl.BlockSpec((None, L, qkw), lambda b, j: (b, row(j), COL_K // qkw)),
                pl.BlockSpec((None, L, vw), lambda b, j: (b, row(j), COL_V // vw)),
                pl.BlockSpec((None, L, LANE), lambda b, j: (b, row(j), 0)),
                pl.BlockSpec((None, N_GATE, L), lambda b, j: (b, 0, row(j)))]

    fwd = lambda j: j
    bwd = lambda j: nc - 1 - j
    return pl.pallas_call(
        _mlstm_kernel,
        grid=(bsz, nc),
        in_specs=specs(fwd) + specs(bwd),
        out_specs=[pl.BlockSpec((None, L, vw), lambda b, j: (b, j, 0)),
                   pl.BlockSpec((None, L, vw), lambda b, j: (b, nc - 1 - j, 0))],
        out_shape=[jax.ShapeDtypeStruct((bsz, s, vw), F32)] * 2,
        scratch_shapes=[pltpu.VMEM((2 * N_HEADS, DQK, DV), F32),
                        pltpu.VMEM((2 * N_HEADS, 1, DQK), F32),
                        pltpu.VMEM((2 * N_HEADS, 1, LANE), F32)],
        compiler_params=_cparams(("arbitrary", "arbitrary")),
        name="mlstm",
    )(proj, proj, proj, gates, gates_t, proj, proj, proj, gates, gates_t)


def _mixout_kernel(gb_ref, gc_ref, hin_ref, gcp_ref, hinp_ref, gcn_ref, hinn_ref, hf_ref, hb_ref, o_ref,
                   x_ref, g1_ref, cw_ref, cb_ref, nw_ref, wout_ref, lw_ref, lb_ref, x1_ref, y_scr):
    i = pl.program_id(1)
    tm = x_ref.shape[0]
    z = gc_ref[...] * hin_ref[...]
    z_before = (gcp_ref[...] * hinp_ref[...])[SUBLANE - 1:SUBLANE, :]
    z_after = (gcn_ref[...] * hinn_ref[...])[0:1, :]
    z_before = jnp.where(i == 0, 0.0, z_before)
    z_after = jnp.where(i == pl.num_programs(1) - 1, 0.0, z_after)
    row = lax.broadcasted_iota(jnp.int32, z.shape, 0)
    z_prev = jnp.where(row == 0, z_before, pltpu.roll(z, 1, axis=0))
    z_next = jnp.where(row == tm - 1, z_after, pltpu.roll(z, tm - 1, axis=0))
    cw = cw_ref[...]
    conv = cw[0:1, :] * z_prev + cw[1:2, :] * z + cw[2:3, :] * z_next + cb_ref[...]
    y_scr[:, 0:CONV_W] = (gb_ref[...] * conv).astype(BF16)
    for h in range(N_HEADS):
        sl = slice(h * DV, (h + 1) * DV)
        hn = _ln(hf_ref[:, sl] + hb_ref[:, sl]) * nw_ref[:, sl]
        y_scr[:, CONV_W + h * DV:CONV_W + (h + 1) * DV] = (jax.nn.sigmoid(o_ref[:, sl]) * hn).astype(BF16)
    y = _bdot(y_scr[...], wout_ref[...])
    x1_ref[...] = _ln(ALPHA * x_ref[...] + g1_ref[...] * y) * lw_ref[...] + lb_ref[...]


def _mixout(proj, hf, hb, x, g1, conv_w, conv_b, norm_w, w_out, ln_w, ln_b):
    bsz, s, _ = x.shape
    tm = min(MIX_TM, s)
    nt = s // tm
    rb = tm // SUBLANE
    last_rb = s // SUBLANE - 1

    def col(width, start):
        return pl.BlockSpec((None, tm, width), lambda b, i: (b, i, start // width))

    def halo_prev(start):
        return pl.BlockSpec((None, SUBLANE, CONV_W),
                            lambda b, i: (b, jnp.maximum(i * rb - 1, 0), start // CONV_W))

    def halo_next(start):
        return pl.BlockSpec((None, SUBLANE, CONV_W),
                            lambda b, i: (b, jnp.minimum((i + 1) * rb, last_rb), start // CONV_W))

    full = lambda shape: pl.BlockSpec(shape, lambda b, i: tuple(0 for _ in shape))
    return pl.pallas_call(
        _mixout_kernel,
        grid=(bsz, nt),
        in_specs=[col(CONV_W, COL_GB), col(CONV_W, COL_GC), col(CONV_W, COL_HIN),
                  halo_prev(COL_GC), halo_prev(COL_HIN), halo_next(COL_GC), halo_next(COL_HIN),
                  pl.BlockSpec((None, tm, N_HEADS * DV), lambda b, i: (b, i, 0)),
                  pl.BlockSpec((None, tm, N_HEADS * DV), lambda b, i: (b, i, 0)),
                  col(N_HEADS * DV, COL_O),
                  pl.BlockSpec((None, tm, D), lambda b, i: (b, i, 0)),
                  pl.BlockSpec((None, 1, D), lambda b, i: (b, 0, 0)),
                  full((CONV_K, CONV_W)), full((1, CONV_W)), full((1, N_HEADS * DV)),
                  full((D, D)), full((1, D)), full((1, D))],
        out_specs=pl.BlockSpec((None, tm, D), lambda b, i: (b, i, 0)),
        out_shape=jax.ShapeDtypeStruct((bsz, s, D), F32),
        scratch_shapes=[pltpu.VMEM((tm, D), BF16)],
        compiler_params=_cparams(("arbitrary", "arbitrary")),
        name="mixout",
    )(proj, proj, proj, proj, proj, proj, proj, hf, hb, proj, x, g1,
      conv_w, conv_b, norm_w, w_out, ln_w, ln_b)


def _peerq_kernel(x1_ref, sh_ref, sc_ref, wq_ref, keys_ref, ut_ref, st_ref):
    u = _ln(x1_ref[...]) * (1.0 + sc_ref[...]) + sh_ref[...]
    ut_ref[...] = u.T.astype(BF16)
    qh = _bdot(u.astype(BF16), wq_ref[...]).astype(BF16)
    for hp in range(2 * P_HEADS):
        st_ref[hp] = lax.dot_general(keys_ref[hp], qh[:, hp * N_KEYS:(hp + 1) * N_KEYS],
                                     (((1,), (1,)), ((), ())), preferred_element_type=F32)


def _peerq(x1, sh, sc, wq, keys):
    bsz, s, _ = x1.shape
    tm = min(PQ_TM, s)
    nt = s // tm
    t = bsz * s
    return pl.pallas_call(
        _peerq_kernel,
        grid=(bsz, nt),
        in_specs=[pl.BlockSpec((None, tm, D), lambda b, i: (b, i, 0)),
                  pl.BlockSpec((None, 1, D), lambda b, i: (b, 0, 0)),
                  pl.BlockSpec((None, 1, D), lambda b, i: (b, 0, 0)),
                  pl.BlockSpec((D, D), lambda b, i: (0, 0)),
                  pl.BlockSpec((2 * P_HEADS, N_KEYS, N_KEYS), lambda b, i: (0, 0, 0))],
        out_specs=[pl.BlockSpec((D, tm), lambda b, i: (0, b * nt + i)),
                   pl.BlockSpec((2 * P_HEADS, N_KEYS, tm), lambda b, i: (0, 0, b * nt + i))],
        out_shape=[jax.ShapeDtypeStruct((D, t), BF16),
                   jax.ShapeDtypeStruct((2 * P_HEADS, N_KEYS, t), F32)],
        compiler_params=_cparams(("arbitrary", "arbitrary")),
        name="peerq",
    )(x1, sh, sc, wq, keys)


def _take_top(x, count):
    rows = lax.broadcasted_iota(jnp.int32, (count, x.shape[1]), 0)
    out = jnp.full((count, x.shape[1]), NEG_INF, F32)
    for r in range(count):
        m = jnp.max(x, axis=0, keepdims=True)
        out = jnp.where(rows == r, m, out)
        x = jnp.where(x == m, NEG_INF, x)
    return out


def _candidate_sums(a, b):
    parts = [a[0:1, :] + b]
    for i in range(1, SUBLANE):
        parts.append(a[i:i + 1, :] + b[0:SUBLANE, :])
    parts.append(a[SUBLANE:TOPK, :] + b[0:1, :])
    return jnp.concatenate(parts, axis=0)


def _peersel_kernel(st_ref, t1_ref, t2_ref, tau_ref):
    def head(h, carry):
        s1 = st_ref[2 * h]
        s2 = st_ref[2 * h + 1]
        a = _take_top(s1, TOPK)
        b = _take_top(s2, TOPK)
        a0, b0 = a[0:1, :], b[0:1, :]
        a_sh, b_sh = a - a0, b - b0
        best = _take_top(_candidate_sums(a_sh, b_sh), TOPK)
        log_z = jnp.log(jnp.sum(jnp.exp(best), axis=0, keepdims=True))
        t1_ref[h] = (s1 - a0) - log_z
        t2_ref[h] = s2 - b0
        best_t = _take_top(_candidate_sums(a_sh - log_z, b_sh), TOPK)
        tau_ref[pl.ds(h, 1), :] = best_t[TOPK - 1:TOPK, :]
        return carry

    lax.fori_loop(0, P_HEADS, head, 0)


def _peersel(st):
    t = st.shape[2]
    tm = SEL_TM
    return pl.pallas_call(
        _peersel_kernel,
        grid=(t // tm,),
        in_specs=[pl.BlockSpec((2 * P_HEADS, N_KEYS, tm), lambda i: (0, 0, i))],
        out_specs=[pl.BlockSpec((P_HEADS, N_KEYS, tm), lambda i: (0, 0, i)),
                   pl.BlockSpec((P_HEADS, N_KEYS, tm), lambda i: (0, 0, i)),
                   pl.BlockSpec((P_HEADS, tm), lambda i: (0, i))],
        out_shape=[jax.ShapeDtypeStruct((P_HEADS, N_KEYS, t), F32),
                   jax.ShapeDtypeStruct((P_HEADS, N_KEYS, t), F32),
                   jax.ShapeDtypeStruct((P_HEADS, t), F32)],
        compiler_params=_cparams(("arbitrary",)),
        name="peersel",
    )(st)


def _gelu(x):
    return 0.5 * x * (1.0 + lax.erf(x * (2.0 ** -0.5)))


def _peer_kernel(ut_ref, pu_ref, pvt_ref, t1_ref, t2_ref, tau_ref, x1_ref, g2_ref, lw_ref, lb_ref,
                 out_ref, acc_scr, h_scr, a_scr):
    j = pl.program_id(1)
    eb, tm = h_scr.shape

    @pl.when(j == 0)
    def _():
        acc_scr[...] = jnp.zeros_like(acc_scr)

    h_scr[...] = _bdot(pu_ref[...], ut_ref[...])
    keys_per_step = eb // N_KEYS
    first_key = j * keys_per_step
    group = pl.multiple_of((first_key // SUBLANE) * SUBLANE, SUBLANE)
    offset = first_key - group
    for kk in range(keys_per_step):
        rows = slice(kk * N_KEYS, (kk + 1) * N_KEYS)
        for c in range(tm // LANE):
            cols = slice(c * LANE, (c + 1) * LANE)
            w = jnp.zeros((N_KEYS, LANE), F32)
            for h in range(P_HEADS):
                t1_group = t1_ref[h, pl.ds(group, SUBLANE), cols]
                t1_row = t1_group[kk:kk + 1, :]
                for g in range(1, SUBLANE // keys_per_step):
                    r = g * keys_per_step + kk
                    t1_row = jnp.where(offset == g * keys_per_step, t1_group[r:r + 1, :], t1_row)
                s = t1_row + t2_ref[h, :, cols]
                w = w + jnp.where(s >= tau_ref[h:h + 1, cols], jnp.exp(s), 0.0)
            a_scr[rows, cols] = (_gelu(h_scr[rows, cols]) * w).astype(BF16)
    acc_scr[...] += _bdot(pvt_ref[...], a_scr[...])

    @pl.when(j == pl.num_programs(1) - 1)
    def _():
        y = acc_scr[...].T
        out_ref[...] = _ln(ALPHA * x1_ref[...] + g2_ref[...] * y) * lw_ref[...] + lb_ref[...]


def _peer(ut, pu, pvt, t1, t2, tau, x1, g2, ln_w, ln_b):
    bsz, s, _ = x1.shape
    tm = min(PEER_TM, s)
    nt = s // tm
    eb = PEER_EB
    return pl.pallas_call(
        _peer_kernel,
        grid=(bsz * nt, N_EXPERTS // eb),
        in_specs=[pl.BlockSpec((D, tm), lambda i, j: (0, i)),
                  pl.BlockSpec((eb, D), lambda i, j: (j, 0)),
                  pl.BlockSpec((D, eb), lambda i, j: (0, j)),
                  pl.BlockSpec((P_HEADS, N_KEYS, tm), lambda i, j: (0, 0, i)),
                  pl.BlockSpec((P_HEADS, N_KEYS, tm), lambda i, j: (0, 0, i)),
                  pl.BlockSpec((P_HEADS, tm), lambda i, j: (0, i)),
                  pl.BlockSpec((None, tm, D), lambda i, j: (i // nt, i % nt, 0)),
                  pl.BlockSpec((None, 1, D), lambda i, j: (i // nt, 0, 0)),
                  pl.BlockSpec((1, D), lambda i, j: (0, 0)),
                  pl.BlockSpec((1, D), lambda i, j: (0, 0))],
        out_specs=pl.BlockSpec((None, tm, D), lambda i, j: (i // nt, i % nt, 0)),
        out_shape=jax.ShapeDtypeStruct((bsz, s, D), F32),
        scratch_shapes=[pltpu.VMEM((D, tm), F32),
                        pltpu.VMEM((eb, tm), F32),
                        pltpu.VMEM((eb, tm), BF16)],
        compiler_params=_cparams(("arbitrary", "arbitrary")),
        name="peer",
    )(ut, pu, pvt, t1, t2, tau, x1, g2, ln_w, ln_b)


def _layer(x, mod, w_in, b_in, conv_w, conv_b, norm_w, w_out, ln1_w, ln1_b,
           wq, subkeys, pu, pv, ln2_w, ln2_b):
    bsz = x.shape[0]
    sh1, sc1, g1, sh2, sc2, g2 = [m.reshape(bsz, 1, D) for m in jnp.split(mod, 6, axis=-1)]
    row = lambda v: v.reshape(1, -1)

    w_main = w_in[:, :MAIN_COLS].astype(BF16)
    w_gate = jnp.pad(w_in[:, MAIN_COLS:], ((0, 0), (0, LANE - N_GATE))).astype(BF16)
    b_gate = jnp.pad(b_in[MAIN_COLS:], (0, LANE - N_GATE))
    proj, gates = _inproj(x, sh1, sc1, w_main, row(b_in[:MAIN_COLS]), w_gate, row(b_gate))
    gates_t = jnp.swapaxes(gates[:, :, :N_GATE], 1, 2)
    hf, hb = _mlstm(proj, gates, gates_t)
    x1 = _mixout(proj, hf, hb, x, g1, conv_w, row(conv_b), row(norm_w), w_out.astype(BF16),
                 row(ln1_w), row(ln1_b))
    keys = subkeys.reshape(2 * P_HEADS, N_KEYS, N_KEYS).astype(BF16)
    ut, st = _peerq(x1, sh2, sc2, wq.astype(BF16), keys)
    t1, t2, tau = _peersel(st)
    return _peer(ut, pu.astype(BF16), pv.T.astype(BF16), t1, t2, tau, x1, g2, row(ln2_w), row(ln2_b))


def kernel(x, c, w_ada, b_ada, w_in, b_in, conv_w, conv_b, mlstm_norm_w, w_out, ln1_w, ln1_b,
           peer_wq, peer_subkeys, peer_u, peer_v, ln2_w, ln2_b):
    for l in range(w_ada.shape[0]):
        mod = _ada(c, w_ada[l], b_ada[l])
        x = _layer(x, mod, w_in[l], b_in[l], conv_w[l], conv_b[l], mlstm_norm_w[l], w_out[l],
                   ln1_w[l], ln1_b[l], peer_wq[l], peer_subkeys[l], peer_u[l], peer_v[l],
                   ln2_w[l], ln2_b[l])
    return x
```

```python
import functools

import jax
import jax.numpy as jnp
from jax import lax
from jax.experimental import pallas as pl
from jax.experimental.pallas import tpu as pltpu

D = 2048
CONV_W = 1024
CONV_K = 3
N_HEADS = 4
DQK = 128
DV = 256
N_GATE = 16
MAIN_COLS = 6144
COL_GB, COL_GC, COL_HIN, COL_Q, COL_K, COL_V, COL_O = 0, 1024, 2048, 3072, 3584, 4096, 5120
P_HEADS = 8
N_KEYS = 128
N_EXPERTS = N_KEYS * N_KEYS
TOPK = 16
LN_EPS = 1e-5
DEPTH = 1
ALPHA = (2 * DEPTH) ** 0.25

LANE = 128
SUBLANE = 8
VMEM_LIMIT = 52 * 1024 * 1024

ADA_TN = 1024
INPROJ_TM, INPROJ_TN = 512, 1024
MLSTM_L = 128
MIX_TM = 256
PQ_TM = 256
SEL_TM = 128
PEER_TM, PEER_EB = 512, 512
MXU_WIDTH = 256
PEER_MM_TILE_N = MXU_WIDTH
PEER_MM_TILE_M1 = 128
PEER_MM_TILE_M2 = 256
PEER_GATE_ROWS = 64

F32 = jnp.float32
BF16 = jnp.bfloat16
NEG_INF = float("-inf")


def _cparams(sem, flags=None):
    return pltpu.CompilerParams(dimension_semantics=sem, vmem_limit_bytes=VMEM_LIMIT, flags=flags)


def _ln(x):
    mu = jnp.mean(x, axis=-1, keepdims=True)
    xc = x - mu
    var = jnp.mean(xc * xc, axis=-1, keepdims=True)
    return xc * lax.rsqrt(var + LN_EPS)


def _bdot(a, b):
    return jnp.dot(a, b, preferred_element_type=F32)


def _ada_kernel(c_ref, w_ref, b_ref, o_ref):
    c = c_ref[...]
    cond = c * jax.nn.sigmoid(c)
    o_ref[...] = _bdot(cond.astype(BF16), w_ref[...].astype(BF16)) + b_ref[...]


def _ada(c, w, b):
    bsz = c.shape[0]
    n = w.shape[1]
    return pl.pallas_call(
        _ada_kernel,
        grid=(n // ADA_TN,),
        in_specs=[pl.BlockSpec((bsz, D), lambda j: (0, 0)),
                  pl.BlockSpec((D, ADA_TN), lambda j: (0, j)),
                  pl.BlockSpec((1, ADA_TN), lambda j: (0, j))],
        out_specs=pl.BlockSpec((bsz, ADA_TN), lambda j: (0, j)),
        out_shape=jax.ShapeDtypeStruct((bsz, n), F32),
        compiler_params=_cparams(("arbitrary",)),
        name="ada",
    )(c, w, b.reshape(1, n))


def _inproj_kernel(x_ref, sh_ref, sc_ref, w_ref, b_ref, wg_ref, bg_ref, proj_ref, gate_ref, u_scr):
    @pl.when(pl.program_id(2) == 0)
    def _():
        u = _ln(x_ref[...]) * (1.0 + sc_ref[...]) + sh_ref[...]
        ub = u.astype(BF16)
        u_scr[...] = ub
        gate_ref[...] = _bdot(ub, wg_ref[...]) + bg_ref[...]

    proj_ref[...] = _bdot(u_scr[...], w_ref[...]) + b_ref[...]


def _inproj(x, sh, sc, w_main, b_main, w_gate, b_gate):
    bsz, s, _ = x.shape
    tm, tn = min(INPROJ_TM, s), INPROJ_TN
    return pl.pallas_call(
        _inproj_kernel,
        grid=(bsz, s // tm, MAIN_COLS // tn),
        in_specs=[pl.BlockSpec((None, tm, D), lambda b, i, j: (b, i, 0)),
                  pl.BlockSpec((None, 1, D), lambda b, i, j: (b, 0, 0)),
                  pl.BlockSpec((None, 1, D), lambda b, i, j: (b, 0, 0)),
                  pl.BlockSpec((D, tn), lambda b, i, j: (0, j)),
                  pl.BlockSpec((1, tn), lambda b, i, j: (0, j)),
                  pl.BlockSpec((D, LANE), lambda b, i, j: (0, 0)),
                  pl.BlockSpec((1, LANE), lambda b, i, j: (0, 0))],
        out_specs=[pl.BlockSpec((None, tm, tn), lambda b, i, j: (b, i, j)),
                   pl.BlockSpec((None, tm, LANE), lambda b, i, j: (b, i, 0))],
        out_shape=[jax.ShapeDtypeStruct((bsz, s, MAIN_COLS), F32),
                   jax.ShapeDtypeStruct((bsz, s, LANE), F32)],
        scratch_shapes=[pltpu.VMEM((tm, D), BF16)],
        compiler_params=_cparams(("arbitrary", "arbitrary", "arbitrary")),
        name="inproj",
    )(x, sh, sc, w_main, b_main, w_gate, b_gate)


def _log_sigmoid(x):
    return jnp.minimum(x, 0.0) - jnp.log1p(jnp.exp(-jnp.abs(x)))


def _mlstm_direction(d, q_ref, k_ref, v_ref, gcol_ref, grow_ref, h_ref, c_scr, n_scr, m_scr):
    L = q_ref.shape[0]
    t_idx = lax.broadcasted_iota(jnp.int32, (L, L), 0)
    s_idx = lax.broadcasted_iota(jnp.int32, (L, L), 1)
    mask = (t_idx >= s_idx) if d == 0 else (t_idx <= s_idx)
    mask_f = mask.astype(F32)
    mask_tf = ((t_idx <= s_idx) if d == 0 else (t_idx >= s_idx)).astype(F32)
    gcol = gcol_ref[...]
    grow = grow_ref[...]
    hi = lax.Precision.HIGHEST
    bcols = jnp.dot(mask_f, _log_sigmoid(gcol), precision=hi, preferred_element_type=F32)
    brows = jnp.dot(_log_sigmoid(grow), mask_tf, precision=hi, preferred_element_type=F32)
    last = L - 1 if d == 0 else 0
    for h in range(N_HEADS):
        idx = d * N_HEADS + h
        ci, cf = d * 8 + h, d * 8 + 4 + h
        b_col = bcols[:, cf:cf + 1]
        b_row = brows[cf:cf + 1, :]
        ig_col = gcol[:, ci:ci + 1]
        ig_row = grow[ci:ci + 1, :]
        b_tot = brows[cf:cf + 1, last:last + 1]
        m_prev = m_scr[idx][:, 0:1]
        dm = jnp.where(mask, b_col - b_row + ig_row, NEG_INF)
        inter = b_col + m_prev
        m_t = jnp.maximum(inter, jnp.max(dm, axis=1, keepdims=True))
        w_intra = jnp.exp(dm - m_t)
        w_inter = jnp.exp(inter - m_t)
        q = q_ref[:, h * DQK:(h + 1) * DQK] * (DQK ** -0.5)
        k = k_ref[:, h * DQK:(h + 1) * DQK]
        v = v_ref[:, h * DV:(h + 1) * DV]
        qb, kb = q.astype(BF16), k.astype(BF16)
        qk = lax.dot_general(qb, kb, (((1,), (1,)), ((), ())), preferred_element_type=F32)
        sw = qk * w_intra
        c_prev = c_scr[idx]
        n_prev = n_scr[idx]
        num = _bdot(sw.astype(BF16), v.astype(BF16)) + w_inter * _bdot(qb, c_prev.astype(BF16))
        den = (jnp.sum(sw, axis=1, keepdims=True)
               + w_inter * jnp.sum(q * n_prev, axis=1, keepdims=True))
        h_ref[:, h * DV:(h + 1) * DV] = num / jnp.maximum(jnp.abs(den), jnp.exp(-m_t))
        a_col = b_tot - b_col + ig_col
        m_new = jnp.maximum(b_tot + m_prev, jnp.max(a_col, axis=0, keepdims=True))
        w_a = jnp.exp(a_col - m_new)
        decay = jnp.exp(b_tot + m_prev - m_new)
        kv = lax.dot_general(kb, (w_a * v).astype(BF16), (((0,), (0,)), ((), ())),
                             preferred_element_type=F32)
        c_scr[idx] = decay * c_prev + kv
        n_scr[idx] = decay * n_prev + jnp.sum(w_a * k, axis=0, keepdims=True)
        m_scr[idx] = jnp.broadcast_to(m_new, (1, LANE))


def _mlstm_kernel(qf, kf, vf, gcf, grf, qb, kb, vb, gcb, grb, hf_ref, hb_ref, c_scr, n_scr, m_scr):
    @pl.when(pl.program_id(1) == 0)
    def _():
        c_scr[...] = jnp.zeros_like(c_scr)
        n_scr[...] = jnp.zeros_like(n_scr)
        m_scr[...] = jnp.zeros_like(m_scr)

    _mlstm_direction(0, qf, kf, vf, gcf, grf, hf_ref, c_scr, n_scr, m_scr)
    _mlstm_direction(1, qb, kb, vb, gcb, grb, hb_ref, c_scr, n_scr, m_scr)


def _mlstm(proj, gates, gates_t):
    bsz, s, _ = proj.shape
    L = min(MLSTM_L, s)
    nc = s // L
    qkw, vw = N_HEADS * DQK, N_HEADS * DV

    def specs(row):
        return [pl.BlockSpec((None, L, qkw), lambda b, j: (b, row(j), COL_Q // qkw)),
                pl.BlockSpec((None, L, qkw), lambda b, j: (b, row(j), COL_K // qkw)),
                pl.BlockSpec((None, L, vw), lambda b, j: (b, row(j), COL_V // vw)),
                pl.BlockSpec((None, L, LANE), lambda b, j: (b, row(j), 0)),
                pl.BlockSpec((None, N_GATE, L), lambda b, j: (b, 0, row(j)))]

    fwd = lambda j: j
    bwd = lambda j: nc - 1 - j
    return pl.pallas_call(
        _mlstm_kernel,
        grid=(bsz, nc),
        in_specs=specs(fwd) + specs(bwd),
        out_specs=[pl.BlockSpec((None, L, vw), lambda b, j: (b, j, 0)),
                   pl.BlockSpec((None, L, vw), lambda b, j: (b, nc - 1 - j, 0))],
        out_shape=[jax.ShapeDtypeStruct((bsz, s, vw), F32)] * 2,
        scratch_shapes=[pltpu.VMEM((2 * N_HEADS, DQK, DV), F32),
                        pltpu.VMEM((2 * N_HEADS, 1, DQK), F32),
                        pltpu.VMEM((2 * N_HEADS, 1, LANE), F32)],
        compiler_params=_cparams(("arbitrary", "arbitrary")),
        name="mlstm",
    )(proj, proj, proj, gates, gates_t, proj, proj, proj, gates, gates_t)


def _mixout_kernel(gb_ref, gc_ref, hin_ref, gcp_ref, hinp_ref, gcn_ref, hinn_ref, hf_ref, hb_ref, o_ref,
                   x_ref, g1_ref, cw_ref, cb_ref, nw_ref, wout_ref, lw_ref, lb_ref, x1_ref, y_scr):
    i = pl.program_id(1)
    tm = x_ref.shape[0]
    z = gc_ref[...] * hin_ref[...]
    z_before = (gcp_ref[...] * hinp_ref[...])[SUBLANE - 1:SUBLANE, :]
    z_after = (gcn_ref[...] * hinn_ref[...])[0:1, :]
    z_before = jnp.where(i == 0, 0.0, z_before)
    z_after = jnp.where(i == pl.num_programs(1) - 1, 0.0, z_after)
    row = lax.broadcasted_iota(jnp.int32, z.shape, 0)
    z_prev = jnp.where(row == 0, z_before, pltpu.roll(z, 1, axis=0))
    z_next = jnp.where(row == tm - 1, z_after, pltpu.roll(z, tm - 1, axis=0))
    cw = cw_ref[...]
    conv = cw[0:1, :] * z_prev + cw[1:2, :] * z + cw[2:3, :] * z_next + cb_ref[...]
    y_scr[:, 0:CONV_W] = (gb_ref[...] * conv).astype(BF16)
    for h in range(N_HEADS):
        sl = slice(h * DV, (h + 1) * DV)
        hn = _ln(hf_ref[:, sl] + hb_ref[:, sl]) * nw_ref[:, sl]
        y_scr[:, CONV_W + h * DV:CONV_W + (h + 1) * DV] = (jax.nn.sigmoid(o_ref[:, sl]) * hn).astype(BF16)
    y = _bdot(y_scr[...], wout_ref[...])
    x1_ref[...] = _ln(ALPHA * x_ref[...] + g1_ref[...] * y) * lw_ref[...] + lb_ref[...]


def _mixout(proj, hf, hb, x, g1, conv_w, conv_b, norm_w, w_out, ln_w, ln_b):
    bsz, s, _ = x.shape
    tm = min(MIX_TM, s)
    nt = s // tm
    rb = tm // SUBLANE
    last_rb = s // SUBLANE - 1

    def col(width, start):
        return pl.BlockSpec((None, tm, width), lambda b, i: (b, i, start // width))

    def halo_prev(start):
        return pl.BlockSpec((None, SUBLANE, CONV_W),
                            lambda b, i: (b, jnp.maximum(i * rb - 1, 0), start // CONV_W))

    def halo_next(start):
        return pl.BlockSpec((None, SUBLANE, CONV_W),
                            lambda b, i: (b, jnp.minimum((i + 1) * rb, last_rb), start // CONV_W))

    full = lambda shape: pl.BlockSpec(shape, lambda b, i: tuple(0 for _ in shape))
    return pl.pallas_call(
        _mixout_kernel,
        grid=(bsz, nt),
        in_specs=[col(CONV_W, COL_GB), col(CONV_W, COL_GC), col(CONV_W, COL_HIN),
                  halo_prev(COL_GC), halo_prev(COL_HIN), halo_next(COL_GC), halo_next(COL_HIN),
                  pl.BlockSpec((None, tm, N_HEADS * DV), lambda b, i: (b, i, 0)),
                  pl.BlockSpec((None, tm, N_HEADS * DV), lambda b, i: (b, i, 0)),
                  col(N_HEADS * DV, COL_O),
                  pl.BlockSpec((None, tm, D), lambda b, i: (b, i, 0)),
                  pl.BlockSpec((None, 1, D), lambda b, i: (b, 0, 0)),
                  full((CONV_K, CONV_W)), full((1, CONV_W)), full((1, N_HEADS * DV)),
                  full((D, D)), full((1, D)), full((1, D))],
        out_specs=pl.BlockSpec((None, tm, D), lambda b, i: (b, i, 0)),
        out_shape=jax.ShapeDtypeStruct((bsz, s, D), F32),
        scratch_shapes=[pltpu.VMEM((tm, D), BF16)],
        compiler_params=_cparams(("arbitrary", "arbitrary")),
        name="mixout",
    )(proj, proj, proj, proj, proj, proj, proj, hf, hb, proj, x, g1,
      conv_w, conv_b, norm_w, w_out, ln_w, ln_b)


def _peerq_kernel(x1_ref, sh_ref, sc_ref, wq_ref, keys_ref, ut_ref, st_ref):
    u = _ln(x1_ref[...]) * (1.0 + sc_ref[...]) + sh_ref[...]
    ut_ref[...] = u.T.astype(BF16)
    qh = _bdot(u.astype(BF16), wq_ref[...]).astype(BF16)
    for hp in range(2 * P_HEADS):
        st_ref[hp] = lax.dot_general(keys_ref[hp], qh[:, hp * N_KEYS:(hp + 1) * N_KEYS],
                                     (((1,), (1,)), ((), ())), preferred_element_type=F32)


def _peerq(x1, sh, sc, wq, keys):
    bsz, s, _ = x1.shape
    tm = min(PQ_TM, s)
    nt = s // tm
    t = bsz * s
    return pl.pallas_call(
        _peerq_kernel,
        grid=(bsz, nt),
        in_specs=[pl.BlockSpec((None, tm, D), lambda b, i: (b, i, 0)),
                  pl.BlockSpec((None, 1, D), lambda b, i: (b, 0, 0)),
                  pl.BlockSpec((None, 1, D), lambda b, i: (b, 0, 0)),
                  pl.BlockSpec((D, D), lambda b, i: (0, 0)),
                  pl.BlockSpec((2 * P_HEADS, N_KEYS, N_KEYS), lambda b, i: (0, 0, 0))],
        out_specs=[pl.BlockSpec((D, tm), lambda b, i: (0, b * nt + i)),
                   pl.BlockSpec((2 * P_HEADS, N_KEYS, tm), lambda b, i: (0, 0, b * nt + i))],
        out_shape=[jax.ShapeDtypeStruct((D, t), BF16),
                   jax.ShapeDtypeStruct((2 * P_HEADS, N_KEYS, t), F32)],
        compiler_params=_cparams(("arbitrary", "arbitrary")),
        name="peerq",
    )(x1, sh, sc, wq, keys)


def _take_top(x, count):
    rows = lax.broadcasted_iota(jnp.int32, (count, x.shape[1]), 0)
    out = jnp.full((count, x.shape[1]), NEG_INF, F32)
    for r in range(count):
        m = jnp.max(x, axis=0, keepdims=True)
        out = jnp.where(rows == r, m, out)
        x = jnp.where(x == m, NEG_INF, x)
    return out


def _candidate_sums(a, b):
    parts = [a[0:1, :] + b]
    for i in range(1, SUBLANE):
        parts.append(a[i:i + 1, :] + b[0:SUBLANE, :])
    parts.append(a[SUBLANE:TOPK, :] + b[0:1, :])
    return jnp.concatenate(parts, axis=0)


def _peersel_kernel(st_ref, t1_ref, t2_ref, tau_ref):
    def head(h, carry):
        s1 = st_ref[2 * h]
        s2 = st_ref[2 * h + 1]
        a = _take_top(s1, TOPK)
        b = _take_top(s2, TOPK)
        a0, b0 = a[0:1, :], b[0:1, :]
        a_sh, b_sh = a - a0, b - b0
        best = _take_top(_candidate_sums(a_sh, b_sh), TOPK)
        log_z = jnp.log(jnp.sum(jnp.exp(best), axis=0, keepdims=True))
        t1_ref[h] = (s1 - a0) - log_z
        t2_ref[h] = s2 - b0
        best_t = _take_top(_candidate_sums(a_sh - log_z, b_sh), TOPK)
        tau_ref[pl.ds(h, 1), :] = best_t[TOPK - 1:TOPK, :]
        return carry

    lax.fori_loop(0, P_HEADS, head, 0)


def _peersel(st):
    t = st.shape[2]
    tm = SEL_TM
    return pl.pallas_call(
        _peersel_kernel,
        grid=(t // tm,),
        in_specs=[pl.BlockSpec((2 * P_HEADS, N_KEYS, tm), lambda i: (0, 0, i))],
        out_specs=[pl.BlockSpec((P_HEADS, N_KEYS, tm), lambda i: (0, 0, i)),
                   pl.BlockSpec((P_HEADS, N_KEYS, tm), lambda i: (0, 0, i)),
                   pl.BlockSpec((P_HEADS, tm), lambda i: (0, i))],
        out_shape=[jax.ShapeDtypeStruct((P_HEADS, N_KEYS, t), F32),
                   jax.ShapeDtypeStruct((P_HEADS, N_KEYS, t), F32),
                   jax.ShapeDtypeStruct((P_HEADS, t), F32)],
        compiler_params=_cparams(("arbitrary",)),
        name="peersel",
    )(st)


def _gelu(x):
    return 0.5 * x * (1.0 + lax.erf(x * (2.0 ** -0.5)))


def _peer_gate_tiles(block, h_ref, a_ref, t1_ref, t2_ref, tau_ref):
    eb, tm = h_ref.shape
    keys_per_step = eb // N_KEYS
    first_key = block * keys_per_step
    group = pl.multiple_of((first_key // SUBLANE) * SUBLANE, SUBLANE)
    offset = first_key - group

    def tile(kk, part, c):
        keys2 = slice(part * PEER_GATE_ROWS, (part + 1) * PEER_GATE_ROWS)
        rows = slice(kk * N_KEYS + keys2.start, kk * N_KEYS + keys2.stop)
        cols = slice(c * LANE, (c + 1) * LANE)
        w = jnp.zeros((PEER_GATE_ROWS, LANE), F32)
        for h in range(P_HEADS):
            t1_group = t1_ref[h, pl.ds(group, SUBLANE), cols]
            t1_row = t1_group[kk:kk + 1, :]
            for g in range(1, SUBLANE // keys_per_step):
                r = g * keys_per_step + kk
                t1_row = jnp.where(offset == g * keys_per_step, t1_group[r:r + 1, :], t1_row)
            s = t1_row + t2_ref[h, keys2, cols]
            w = w + jnp.where(s >= tau_ref[h:h + 1, cols], jnp.exp(s), 0.0)
        a_ref[rows, cols] = (_gelu(h_ref[rows, cols]) * w).astype(BF16)

    return [functools.partial(tile, kk, part, c) for kk in range(keys_per_step)
            for part in range(N_KEYS // PEER_GATE_ROWS) for c in range(tm // LANE)]


def _matmul_pieces(out_ref, lhs_ref, rhs_ref, tile_m, tile_n, accumulate):
    def piece(m, n):
        rows = slice(m * tile_m, (m + 1) * tile_m)
        cols = slice(n * tile_n, (n + 1) * tile_n)
        res = _bdot(lhs_ref[rows, :], rhs_ref[:, cols])
        out_ref[rows, cols] = out_ref[rows, cols] + res if accumulate else res

    return [functools.partial(piece, m, n)
            for n in range(out_ref.shape[1] // tile_n) for m in range(out_ref.shape[0] // tile_m)]


def _peer_kernel(ut_ref, pu_ref, pvt_ref, t1_ref, t2_ref, tau_ref, x1_ref, g2_ref, lw_ref, lb_ref,
                 out_ref, acc_scr, h0_scr, h1_scr, a0_scr, a1_scr):
    s = pl.program_id(1)
    n_blocks = pl.num_programs(1) - 2

    @pl.when(s == 0)
    def _():
        acc_scr[...] = jnp.zeros_like(acc_scr)
        h1_scr[...] = jnp.zeros_like(h1_scr)
        a1_scr[...] = jnp.zeros_like(a1_scr)

    def stages(h_write, h_read, a_write, a_read):
        block = jnp.clip(s - 1, 0, n_blocks - 1)
        gate = _peer_gate_tiles(block, h_read, a_write, t1_ref, t2_ref, tau_ref)
        mxu = (_matmul_pieces(h_write, pu_ref, ut_ref, PEER_MM_TILE_M1, PEER_MM_TILE_N, False)
               + _matmul_pieces(acc_scr, pvt_ref, a_read, PEER_MM_TILE_M2, PEER_MM_TILE_N, True))
        issued = 0
        for i, gate_tile in enumerate(gate):
            ahead = -(-(i + 1) * len(mxu) // len(gate))
            for piece in mxu[issued:ahead]:
                piece()
            issued = ahead
            gate_tile()

    @pl.when(s % 2 == 0)
    def _():
        stages(h0_scr, h1_scr, a0_scr, a1_scr)

    @pl.when(s % 2 == 1)
    def _():
        stages(h1_scr, h0_scr, a1_scr, a0_scr)

    @pl.when(s == pl.num_programs(1) - 1)
    def _():
        y = acc_scr[...].T
        out_ref[...] = _ln(ALPHA * x1_ref[...] + g2_ref[...] * y) * lw_ref[...] + lb_ref[...]


def _peer(ut, pu, pvt, t1, t2, tau, x1, g2, ln_w, ln_b):
    bsz, s, _ = x1.shape
    tm = min(PEER_TM, s)
    nt = s // tm
    eb = PEER_EB
    nb = N_EXPERTS // eb
    return pl.pallas_call(
        _peer_kernel,
        grid=(bsz * nt, nb + 2),
        in_specs=[pl.BlockSpec((D, tm), lambda i, j: (0, i)),
                  pl.BlockSpec((eb, D), lambda i, j: (jnp.minimum(j, nb - 1), 0)),
                  pl.BlockSpec((D, eb), lambda i, j: (0, jnp.clip(j - 2, 0, nb - 1))),
                  pl.BlockSpec((P_HEADS, N_KEYS, tm), lambda i, j: (0, 0, i)),
                  pl.BlockSpec((P_HEADS, N_KEYS, tm), lambda i, j: (0, 0, i)),
                  pl.BlockSpec((P_HEADS, tm), lambda i, j: (0, i)),
                  pl.BlockSpec((None, tm, D), lambda i, j: (i // nt, i % nt, 0)),
                  pl.BlockSpec((None, 1, D), lambda i, j: (i // nt, 0, 0)),
                  pl.BlockSpec((1, D), lambda i, j: (0, 0)),
                  pl.BlockSpec((1, D), lambda i, j: (0, 0))],
        out_specs=pl.BlockSpec((None, tm, D), lambda i, j: (i // nt, i % nt, 0)),
        out_shape=jax.ShapeDtypeStruct((bsz, s, D), F32),
        scratch_shapes=[pltpu.VMEM((D, tm), F32),
                        pltpu.VMEM((eb, tm), F32), pltpu.VMEM((eb, tm), F32),
                        pltpu.VMEM((eb, tm), BF16), pltpu.VMEM((eb, tm), BF16)],
        compiler_params=_cparams(("arbitrary", "arbitrary")),
        name="peer",
    )(ut, pu, pvt, t1, t2, tau, x1, g2, ln_w, ln_b)


def _layer(x, mod, w_in, b_in, conv_w, conv_b, norm_w, w_out, ln1_w, ln1_b,
           wq, subkeys, pu, pv, ln2_w, ln2_b):
    bsz = x.shape[0]
    sh1, sc1, g1, sh2, sc2, g2 = [m.reshape(bsz, 1, D) for m in jnp.split(mod, 6, axis=-1)]
    row = lambda v: v.reshape(1, -1)

    w_main = w_in[:, :MAIN_COLS].astype(BF16)
    w_gate = jnp.pad(w_in[:, MAIN_COLS:], ((0, 0), (0, LANE - N_GATE))).astype(BF16)
    b_gate = jnp.pad(b_in[MAIN_COLS:], (0, LANE - N_GATE))
    proj, gates = _inproj(x, sh1, sc1, w_main, row(b_in[:MAIN_COLS]), w_gate, row(b_gate))
    gates_t = jnp.swapaxes(gates[:, :, :N_GATE], 1, 2)
    hf, hb = _mlstm(proj, gates, gates_t)
    x1 = _mixout(proj, hf, hb, x, g1, conv_w, row(conv_b), row(norm_w), w_out.astype(BF16),
                 row(ln1_w), row(ln1_b))
    keys = subkeys.reshape(2 * P_HEADS, N_KEYS, N_KEYS).astype(BF16)
    ut, st = _peerq(x1, sh2, sc2, wq.astype(BF16), keys)
    t1, t2, tau = _peersel(st)
    return _peer(ut, pu.astype(BF16), pv.T.astype(BF16), t1, t2, tau, x1, g2, row(ln2_w), row(ln2_b))


def kernel(x, c, w_ada, b_ada, w_in, b_in, conv_w, conv_b, mlstm_norm_w, w_out, ln1_w, ln1_b,
           peer_wq, peer_subkeys, peer_u, peer_v, ln2_w, ln2_b):
    for l in range(w_ada.shape[0]):
        mod = _ada(c, w_ada[l], b_ada[l])
        x = _layer(x, mod, w_in[l], b_in[l], conv_w[l], conv_b[l], mlstm_norm_w[l], w_out[l],
                   ln1_w[l], ln1_b[l], peer_wq[l], peer_subkeys[l], peer_u[l], peer_v[l],
                   ln2_w[l], ln2_b[l])
    return x
```

```python
import functools

import jax
import jax.numpy as jnp
from jax import lax
from jax.experimental import pallas as pl
from jax.experimental.pallas import tpu as pltpu

D = 2048
CONV_W = 1024
CONV_K = 3
N_HEADS = 4
DQK = 128
DV = 256
N_GATE = 16
MAIN_COLS = 6144
COL_GB, COL_GC, COL_HIN, COL_Q, COL_K, COL_V, COL_O = 0, 1024, 2048, 3072, 3584, 4096, 5120
P_HEADS = 8
N_KEYS = 128
N_EXPERTS = N_KEYS * N_KEYS
TOPK = 16
LN_EPS = 1e-5
DEPTH = 1
ALPHA = (2 * DEPTH) ** 0.25

LANE = 128
SUBLANE = 8
VMEM_LIMIT = 52 * 1024 * 1024

ADA_TN = 1024
INPROJ_TM, INPROJ_TN = 512, 1024
MLSTM_L = 128
MIX_TM = 256
PQ_TM = 256
SEL_TM = 128
PEER_TM, PEER_EB = 512, 512
MXU_WIDTH = 256
PEER_MM_TILE_N = MXU_WIDTH
PEER_MM_TILE_M = 512
PEER_MM_TILE_K1 = MXU_WIDTH
PEER_GATE_ROWS = 64
PEER_MXU_LEAD = 3

F32 = jnp.float32
BF16 = jnp.bfloat16
NEG_INF = float("-inf")


def _cparams(sem, flags=None):
    return pltpu.CompilerParams(dimension_semantics=sem, vmem_limit_bytes=VMEM_LIMIT, flags=flags)


def _ln(x):
    mu = jnp.mean(x, axis=-1, keepdims=True)
    xc = x - mu
    var = jnp.mean(xc * xc, axis=-1, keepdims=True)
    return xc * lax.rsqrt(var + LN_EPS)


def _bdot(a, b):
    return jnp.dot(a, b, preferred_element_type=F32)


def _ada_kernel(c_ref, w_ref, b_ref, o_ref):
    c = c_ref[...]
    cond = c * jax.nn.sigmoid(c)
    o_ref[...] = _bdot(cond.astype(BF16), w_ref[...].astype(BF16)) + b_ref[...]


def _ada(c, w, b):
    bsz = c.shape[0]
    n = w.shape[1]
    return pl.pallas_call(
        _ada_kernel,
        grid=(n // ADA_TN,),
        in_specs=[pl.BlockSpec((bsz, D), lambda j: (0, 0)),
                  pl.BlockSpec((D, ADA_TN), lambda j: (0, j)),
                  pl.BlockSpec((1, ADA_TN), lambda j: (0, j))],
        out_specs=pl.BlockSpec((bsz, ADA_TN), lambda j: (0, j)),
        out_shape=jax.ShapeDtypeStruct((bsz, n), F32),
        compiler_params=_cparams(("arbitrary",)),
        name="ada",
    )(c, w, b.reshape(1, n))


def _inproj_kernel(x_ref, sh_ref, sc_ref, w_ref, b_ref, wg_ref, bg_ref, proj_ref, gate_ref, u_scr):
    @pl.when(pl.program_id(2) == 0)
    def _():
        u = _ln(x_ref[...]) * (1.0 + sc_ref[...]) + sh_ref[...]
        ub = u.astype(BF16)
        u_scr[...] = ub
        gate_ref[...] = _bdot(ub, wg_ref[...]) + bg_ref[...]

    proj_ref[...] = _bdot(u_scr[...], w_ref[...]) + b_ref[...]


def _inproj(x, sh, sc, w_main, b_main, w_gate, b_gate):
    bsz, s, _ = x.shape
    tm, tn = min(INPROJ_TM, s), INPROJ_TN
    return pl.pallas_call(
        _inproj_kernel,
        grid=(bsz, s // tm, MAIN_COLS // tn),
        in_specs=[pl.BlockSpec((None, tm, D), lambda b, i, j: (b, i, 0)),
                  pl.BlockSpec((None, 1, D), lambda b, i, j: (b, 0, 0)),
                  pl.BlockSpec((None, 1, D), lambda b, i, j: (b, 0, 0)),
                  pl.BlockSpec((D, tn), lambda b, i, j: (0, j)),
                  pl.BlockSpec((1, tn), lambda b, i, j: (0, j)),
                  pl.BlockSpec((D, LANE), lambda b, i, j: (0, 0)),
                  pl.BlockSpec((1, LANE), lambda b, i, j: (0, 0))],
        out_specs=[pl.BlockSpec((None, tm, tn), lambda b, i, j: (b, i, j)),
                   pl.BlockSpec((None, tm, LANE), lambda b, i, j: (b, i, 0))],
        out_shape=[jax.ShapeDtypeStruct((bsz, s, MAIN_COLS), F32),
                   jax.ShapeDtypeStruct((bsz, s, LANE), F32)],
        scratch_shapes=[pltpu.VMEM((tm, D), BF16)],
        compiler_params=_cparams(("arbitrary", "arbitrary", "arbitrary")),
        name="inproj",
    )(x, sh, sc, w_main, b_main, w_gate, b_gate)


def _log_sigmoid(x):
    return jnp.minimum(x, 0.0) - jnp.log1p(jnp.exp(-jnp.abs(x)))


def _mlstm_direction(d, q_ref, k_ref, v_ref, gcol_ref, grow_ref, h_ref, c_scr, n_scr, m_scr):
    L = q_ref.shape[0]
    t_idx = lax.broadcasted_iota(jnp.int32, (L, L), 0)
    s_idx = lax.broadcasted_iota(jnp.int32, (L, L), 1)
    mask = (t_idx >= s_idx) if d == 0 else (t_idx <= s_idx)
    mask_f = mask.astype(F32)
    mask_tf = ((t_idx <= s_idx) if d == 0 else (t_idx >= s_idx)).astype(F32)
    gcol = gcol_ref[...]
    grow = grow_ref[...]
    hi = lax.Precision.HIGHEST
    bcols = jnp.dot(mask_f, _log_sigmoid(gcol), precision=hi, preferred_element_type=F32)
    brows = jnp.dot(_log_sigmoid(grow), mask_tf, precision=hi, preferred_element_type=F32)
    last = L - 1 if d == 0 else 0
    for h in range(N_HEADS):
        idx = d * N_HEADS + h
        ci, cf = d * 8 + h, d * 8 + 4 + h
        b_col = bcols[:, cf:cf + 1]
        b_row = brows[cf:cf + 1, :]
        ig_col = gcol[:, ci:ci + 1]
        ig_row = grow[ci:ci + 1, :]
        b_tot = brows[cf:cf + 1, last:last + 1]
        m_prev = m_scr[idx][:, 0:1]
        dm = jnp.where(mask, b_col - b_row + ig_row, NEG_INF)
        inter = b_col + m_prev
        m_t = jnp.maximum(inter, jnp.max(dm, axis=1, keepdims=True))
        w_intra = jnp.exp(dm - m_t)
        w_inter = jnp.exp(inter - m_t)
        q = q_ref[:, h * DQK:(h + 1) * DQK] * (DQK ** -0.5)
        k = k_ref[:, h * DQK:(h + 1) * DQK]
        v = v_ref[:, h * DV:(h + 1) * DV]
        qb, kb = q.astype(BF16), k.astype(BF16)
        qk = lax.dot_general(qb, kb, (((1,), (1,)), ((), ())), preferred_element_type=F32)
        sw = qk * w_intra
        c_prev = c_scr[idx]
        n_prev = n_scr[idx]
        num = _bdot(sw.astype(BF16), v.astype(BF16)) + w_inter * _bdot(qb, c_prev.astype(BF16))
        den = (jnp.sum(sw, axis=1, keepdims=True)
               + w_inter * jnp.sum(q * n_prev, axis=1, keepdims=True))
        h_ref[:, h * DV:(h + 1) * DV] = num / jnp.maximum(jnp.abs(den), jnp.exp(-m_t))
        a_col = b_tot - b_col + ig_col
        m_new = jnp.maximum(b_tot + m_prev, jnp.max(a_col, axis=0, keepdims=True))
        w_a = jnp.exp(a_col - m_new)
        decay = jnp.exp(b_tot + m_prev - m_new)
        kv = lax.dot_general(kb, (w_a * v).astype(BF16), (((0,), (0,)), ((), ())),
                             preferred_element_type=F32)
        c_scr[idx] = decay * c_prev + kv
        n_scr[idx] = decay * n_prev + jnp.sum(w_a * k, axis=0, keepdims=True)
        m_scr[idx] = jnp.broadcast_to(m_new, (1, LANE))


def _mlstm_kernel(qf, kf, vf, gcf, grf, qb, kb, vb, gcb, grb, hf_ref, hb_ref, c_scr, n_scr, m_scr):
    @pl.when(pl.program_id(1) == 0)
    def _():
        c_scr[...] = jnp.zeros_like(c_scr)
        n_scr[...] = jnp.zeros_like(n_scr)
        m_scr[...] = jnp.zeros_like(m_scr)

    _mlstm_direction(0, qf, kf, vf, gcf, grf, hf_ref, c_scr, n_scr, m_scr)
    _mlstm_direction(1, qb, kb, vb, gcb, grb, hb_ref, c_scr, n_scr, m_scr)


def _mlstm(proj, gates, gates_t):
    bsz, s, _ = proj.shape
    L = min(MLSTM_L, s)
    nc = s // L
    qkw, vw = N_HEADS * DQK, N_HEADS * DV

    def specs(row):
        return [pl.BlockSpec((None, L, qkw), lambda b, j: (b, row(j), COL_Q // qkw)),
                pl.BlockSpec((None, L, qkw), lambda b, j: (b, row(j), COL_K // qkw)),
                pl.BlockSpec((None, L, vw), lambda b, j: (b, row(j), COL_V // vw)),
                pl.BlockSpec((None, L, LANE), lambda b, j: (b, row(j), 0)),
                pl.BlockSpec((None, N_GATE, L), lambda b, j: (b, 0, row(j)))]

    fwd = lambda j: j
    bwd = lambda j: nc - 1 - j
    return pl.pallas_call(
        _mlstm_kernel,
        grid=(bsz, nc),
        in_specs=specs(fwd) + specs(bwd),
        out_specs=[pl.BlockSpec((None, L, vw), lambda b, j: (b, j, 0)),
                   pl.BlockSpec((None, L, vw), lambda b, j: (b, nc - 1 - j, 0))],
        out_shape=[jax.ShapeDtypeStruct((bsz, s, vw), F32)] * 2,
        scratch_shapes=[pltpu.VMEM((2 * N_HEADS, DQK, DV), F32),
                        pltpu.VMEM((2 * N_HEADS, 1, DQK), F32),
                        pltpu.VMEM((2 * N_HEADS, 1, LANE), F32)],
        compiler_params=_cparams(("arbitrary", "arbitrary")),
        name="mlstm",
    )(proj, proj, proj, gates, gates_t, proj, proj, proj, gates, gates_t)


def _mixout_kernel(gb_ref, gc_ref, hin_ref, gcp_ref, hinp_ref, gcn_ref, hinn_ref, hf_ref, hb_ref, o_ref,
                   x_ref, g1_ref, cw_ref, cb_ref, nw_ref, wout_ref, lw_ref, lb_ref, x1_ref, y_scr):
    i = pl.program_id(1)
    tm = x_ref.shape[0]
    z = gc_ref[...] * hin_ref[...]
    z_before = (gcp_ref[...] * hinp_ref[...])[SUBLANE - 1:SUBLANE, :]
    z_after = (gcn_ref[...] * hinn_ref[...])[0:1, :]
    z_before = jnp.where(i == 0, 0.0, z_before)
    z_after = jnp.where(i == pl.num_programs(1) - 1, 0.0, z_after)
    row = lax.broadcasted_iota(jnp.int32, z.shape, 0)
    z_prev = jnp.where(row == 0, z_before, pltpu.roll(z, 1, axis=0))
    z_next = jnp.where(row == tm - 1, z_after, pltpu.roll(z, tm - 1, axis=0))
    cw = cw_ref[...]
    conv = cw[0:1, :] * z_prev + cw[1:2, :] * z + cw[2:3, :] * z_next + cb_ref[...]
    y_scr[:, 0:CONV_W] = (gb_ref[...] * conv).astype(BF16)
    for h in range(N_HEADS):
        sl = slice(h * DV, (h + 1) * DV)
        hn = _ln(hf_ref[:, sl] + hb_ref[:, sl]) * nw_ref[:, sl]
        y_scr[:, CONV_W + h * DV:CONV_W + (h + 1) * DV] = (jax.nn.sigmoid(o_ref[:, sl]) * hn).astype(BF16)
    y = _bdot(y_scr[...], wout_ref[...])
    x1_ref[...] = _ln(ALPHA * x_ref[...] + g1_ref[...] * y) * lw_ref[...] + lb_ref[...]


def _mixout(proj, hf, hb, x, g1, conv_w, conv_b, norm_w, w_out, ln_w, ln_b):
    bsz, s, _ = x.shape
    tm = min(MIX_TM, s)
    nt = s // tm
    rb = tm // SUBLANE
    last_rb = s // SUBLANE - 1

    def col(width, start):
        return pl.BlockSpec((None, tm, width), lambda b, i: (b, i, start // width))

    def halo_prev(start):
        return pl.BlockSpec((None, SUBLANE, CONV_W),
                            lambda b, i: (b, jnp.maximum(i * rb - 1, 0), start // CONV_W))

    def halo_next(start):
        return pl.BlockSpec((None, SUBLANE, CONV_W),
                            lambda b, i: (b, jnp.minimum((i + 1) * rb, last_rb), start // CONV_W))

    full = lambda shape: pl.BlockSpec(shape, lambda b, i: tuple(0 for _ in shape))
    return pl.pallas_call(
        _mixout_kernel,
        grid=(bsz, nt),
        in_specs=[col(CONV_W, COL_GB), col(CONV_W, COL_GC), col(CONV_W, COL_HIN),
                  halo_prev(COL_GC), halo_prev(COL_HIN), halo_next(COL_GC), halo_next(COL_HIN),
                  pl.BlockSpec((None, tm, N_HEADS * DV), lambda b, i: (b, i, 0)),
                  pl.BlockSpec((None, tm, N_HEADS * DV), lambda b, i: (b, i, 0)),
                  col(N_HEADS * DV, COL_O),
                  pl.BlockSpec((None, tm, D), lambda b, i: (b, i, 0)),
                  pl.BlockSpec((None, 1, D), lambda b, i: (b, 0, 0)),
                  full((CONV_K, CONV_W)), full((1, CONV_W)), full((1, N_HEADS * DV)),
                  full((D, D)), full((1, D)), full((1, D))],
        out_specs=pl.BlockSpec((None, tm, D), lambda b, i: (b, i, 0)),
        out_shape=jax.ShapeDtypeStruct((bsz, s, D), F32),
        scratch_shapes=[pltpu.VMEM((tm, D), BF16)],
        compiler_params=_cparams(("arbitrary", "arbitrary")),
        name="mixout",
    )(proj, proj, proj, proj, proj, proj, proj, hf, hb, proj, x, g1,
      conv_w, conv_b, norm_w, w_out, ln_w, ln_b)


def _peerq_kernel(x1_ref, sh_ref, sc_ref, wq_ref, keys_ref, ut_ref, st_ref):
    u = _ln(x1_ref[...]) * (1.0 + sc_ref[...]) + sh_ref[...]
    ut_ref[...] = u.T.astype(BF16)
    qh = _bdot(u.astype(BF16), wq_ref[...]).astype(BF16)
    for hp in range(2 * P_HEADS):
        st_ref[hp] = lax.dot_general(keys_ref[hp], qh[:, hp * N_KEYS:(hp + 1) * N_KEYS],
                                     (((1,), (1,)), ((), ())), preferred_element_type=F32)


def _peerq(x1, sh, sc, wq, keys):
    bsz, s, _ = x1.shape
    tm = min(PQ_TM, s)
    nt = s // tm
    t = bsz * s
    return pl.pallas_call(
        _peerq_kernel,
        grid=(bsz, nt),
        in_specs=[pl.BlockSpec((None, tm, D), lambda b, i: (b, i, 0)),
                  pl.BlockSpec((None, 1, D), lambda b, i: (b, 0, 0)),
                  pl.BlockSpec((None, 1, D), lambda b, i: (b, 0, 0)),
                  pl.BlockSpec((D, D), lambda b, i: (0, 0)),
                  pl.BlockSpec((2 * P_HEADS, N_KEYS, N_KEYS), lambda b, i: (0, 0, 0))],
        out_specs=[pl.BlockSpec((D, tm), lambda b, i: (0, b * nt + i)),
                   pl.BlockSpec((2 * P_HEADS, N_KEYS, tm), lambda b, i: (0, 0, b * nt + i))],
        out_shape=[jax.ShapeDtypeStruct((D, t), BF16),
                   jax.ShapeDtypeStruct((2 * P_HEADS, N_KEYS, t), F32)],
        compiler_params=_cparams(("arbitrary", "arbitrary")),
        name="peerq",
    )(x1, sh, sc, wq, keys)


def _sort_network(n):
    pairs = []
    p = 1
    while p < n:
        k = p
        while k >= 1:
            for j in range(k % p, n - k, 2 * k):
                for i in range(min(k, n - j - k)):
                    if (i + j) // (2 * p) == (i + j + k) // (2 * p):
                        pairs.append((i + j, i + j + k))
            k //= 2
        p *= 2
    return pairs


def _exchange(v, i, j):
    if v[j] is None:
        return
    if v[i] is None:
        v[i], v[j] = v[j], None
        return
    v[i], v[j] = jnp.maximum(v[i], v[j]), jnp.minimum(v[i], v[j])


def _top_sorted(v):
    v = list(v)
    for i, j in _sort_network(TOPK):
        _exchange(v, i, j)
    shift = SUBLANE // 2
    while shift >= 1:
        other = [None if x is None else pltpu.roll(x, shift, axis=0) for x in v]
        merged = []
        for r in range(TOPK):
            x, y = v[r], other[TOPK - 1 - r]
            merged.append(y if x is None else x if y is None else jnp.maximum(x, y))
        v = merged
        d = TOPK // 2
        while d >= 1:
            for i in range(TOPK):
                if i & d == 0:
                    _exchange(v, i, i + d)
            d //= 2
        shift //= 2
    return v


def _pack_rows(v):
    sub = lax.broadcasted_iota(jnp.int32, v[0].shape, 0)
    groups = []
    for g in range(0, len(v), SUBLANE):
        x = v[g]
        for r in range(1, SUBLANE):
            x = jnp.where(sub == r, v[g + r], x)
        groups.append(x)
    return groups


def _candidate_sums(a, b):
    b_lo, b_hi = _pack_rows(b)
    a_hi = _pack_rows(a)[1]
    parts = [a[0] + b_lo, a[0] + b_hi]
    parts += [a[i] + b_lo for i in range(1, SUBLANE)]
    parts.append(a_hi + b[0])
    return parts + [None] * (TOPK - len(parts))


def _peersel_kernel(st_ref, t1_ref, t2_ref, tau_ref):
    log2e = 1.4426950408889634

    def head(h, carry):
        s1 = st_ref[2 * h]
        s2 = st_ref[2 * h + 1]
        split = lambda x: [x[r * SUBLANE:(r + 1) * SUBLANE, :] for r in range(N_KEYS // SUBLANE)]
        a = _top_sorted(split(s1))
        b = _top_sorted(split(s2))
        a0, b0 = a[0][0:1, :], b[0][0:1, :]
        a_sh = [x - a0 for x in a]
        b_sh = [x - b0 for x in b]
        best = _top_sorted(_candidate_sums(a_sh, b_sh))
        z = jnp.exp(best[0])
        for x in best[1:]:
            z = z + jnp.exp(x)
        log_z = jnp.log(z[0:1, :])
        t1_ref[h] = ((s1 - a0) - log_z) * log2e
        t2_ref[h] = (s2 - b0) * log2e
        best_t = _top_sorted(_candidate_sums([(x - log_z) * log2e for x in a_sh],
                                             [x * log2e for x in b_sh]))
        tau_ref[pl.ds(h, 1), :] = best_t[TOPK - 1][0:1, :]
        return carry

    lax.fori_loop(0, P_HEADS, head, 0)


def _peersel(st):
    t = st.shape[2]
    tm = SEL_TM
    return pl.pallas_call(
        _peersel_kernel,
        grid=(t // tm,),
        in_specs=[pl.BlockSpec((2 * P_HEADS, N_KEYS, tm), lambda i: (0, 0, i))],
        out_specs=[pl.BlockSpec((P_HEADS, N_KEYS, tm), lambda i: (0, 0, i)),
                   pl.BlockSpec((P_HEADS, N_KEYS, tm), lambda i: (0, 0, i)),
                   pl.BlockSpec((P_HEADS, tm), lambda i: (0, i))],
        out_shape=[jax.ShapeDtypeStruct((P_HEADS, N_KEYS, t), F32),
                   jax.ShapeDtypeStruct((P_HEADS, N_KEYS, t), F32),
                   jax.ShapeDtypeStruct((P_HEADS, t), F32)],
        compiler_params=_cparams(("arbitrary",)),
        name="peersel",
    )(st)


def _gelu(x):
    return 0.5 * x * (1.0 + lax.erf(x * (2.0 ** -0.5)))


def _peer_gate_tiles(block, h_ref, a_ref, t1_ref, t2_ref, tau_ref):
    eb, tm = h_ref.shape
    keys_per_step = eb // N_KEYS
    first_key = block * keys_per_step
    group = pl.multiple_of((first_key // SUBLANE) * SUBLANE, SUBLANE)
    offset = first_key - group

    def tile(kk, part, c):
        keys2 = slice(part * PEER_GATE_ROWS, (part + 1) * PEER_GATE_ROWS)
        rows = slice(kk * N_KEYS + keys2.start, kk * N_KEYS + keys2.stop)
        cols = slice(c * LANE, (c + 1) * LANE)
        w = jnp.zeros((PEER_GATE_ROWS, LANE), F32)
        for h in range(P_HEADS):
            t1_group = t1_ref[h, pl.ds(group, SUBLANE), cols]
            t1_row = t1_group[kk:kk + 1, :]
            for g in range(1, SUBLANE // keys_per_step):
                r = g * keys_per_step + kk
                t1_row = jnp.where(offset == g * keys_per_step, t1_group[r:r + 1, :], t1_row)
            s = t1_row + t2_ref[h, keys2, cols]
            w = w + jnp.where(s >= tau_ref[h:h + 1, cols], jnp.exp2(s), 0.0)
        a_ref[rows, cols] = (_gelu(h_ref[rows, cols]) * w).astype(BF16)

    return [functools.partial(tile, kk, part, c) for kk in range(keys_per_step)
            for part in range(N_KEYS // PEER_GATE_ROWS) for c in range(tm // LANE)]


def _matmul_pieces(out_ref, lhs_ref, rhs_ref, tile_m, tile_n, tile_k, accumulate):
    def piece(m, n, k):
        rows = slice(m * tile_m, (m + 1) * tile_m)
        cols = slice(n * tile_n, (n + 1) * tile_n)
        depth = slice(k * tile_k, (k + 1) * tile_k)
        res = _bdot(lhs_ref[rows, depth], rhs_ref[depth, cols])
        out_ref[rows, cols] = out_ref[rows, cols] + res if (accumulate or k > 0) else res

    return [functools.partial(piece, m, n, k)
            for n in range(out_ref.shape[1] // tile_n) for m in range(out_ref.shape[0] // tile_m)
            for k in range(lhs_ref.shape[1] // tile_k)]


def _peer_kernel(ut_ref, pu_ref, pvt_ref, t1_ref, t2_ref, tau_ref, x1_ref, g2_ref, lw_ref, lb_ref,
                 out_ref, acc_scr, h0_scr, h1_scr, a0_scr, a1_scr):
    s = pl.program_id(1)
    n_blocks = pl.num_programs(1) - 2

    @pl.when(s == 0)
    def _():
        acc_scr[...] = jnp.zeros_like(acc_scr)
        h1_scr[...] = jnp.zeros_like(h1_scr)
        a1_scr[...] = jnp.zeros_like(a1_scr)

    def stages(h_write, h_read, a_write, a_read):
        block = jnp.clip(s - 1, 0, n_blocks - 1)
        gate = _peer_gate_tiles(block, h_read, a_write, t1_ref, t2_ref, tau_ref)
        mxu = (_matmul_pieces(h_write, pu_ref, ut_ref, PEER_MM_TILE_M, PEER_MM_TILE_N, PEER_MM_TILE_K1, False)
               + _matmul_pieces(acc_scr, pvt_ref, a_read, PEER_MM_TILE_M, PEER_MM_TILE_N, a_read.shape[0], True))
        issued = 0
        for i, gate_tile in enumerate(gate):
            ahead = min(len(mxu), -(-(i + 1 + PEER_MXU_LEAD) * len(mxu) // len(gate)))
            for piece in mxu[issued:ahead]:
                piece()
            issued = ahead
            gate_tile()

    @pl.when(s % 2 == 0)
    def _():
        stages(h0_scr, h1_scr, a0_scr, a1_scr)

    @pl.when(s % 2 == 1)
    def _():
        stages(h1_scr, h0_scr, a1_scr, a0_scr)

    @pl.when(s == pl.num_programs(1) - 1)
    def _():
        y = acc_scr[...].T
        out_ref[...] = _ln(ALPHA * x1_ref[...] + g2_ref[...] * y) * lw_ref[...] + lb_ref[...]


def _peer(ut, pu, pvt, t1, t2, tau, x1, g2, ln_w, ln_b):
    bsz, s, _ = x1.shape
    tm = min(PEER_TM, s)
    nt = s // tm
    eb = PEER_EB
    nb = N_EXPERTS // eb
    return pl.pallas_call(
        _peer_kernel,
        grid=(bsz * nt, nb + 2),
        in_specs=[pl.BlockSpec((D, tm), lambda i, j: (0, i)),
                  pl.BlockSpec((eb, D), lambda i, j: (jnp.minimum(j, nb - 1), 0)),
                  pl.BlockSpec((None, D, eb), lambda i, j: (jnp.clip(j - 2, 0, nb - 1), 0, 0)),
                  pl.BlockSpec((P_HEADS, N_KEYS, tm), lambda i, j: (0, 0, i)),
                  pl.BlockSpec((P_HEADS, N_KEYS, tm), lambda i, j: (0, 0, i)),
                  pl.BlockSpec((P_HEADS, tm), lambda i, j: (0, i)),
                  pl.BlockSpec((None, tm, D), lambda i, j: (i // nt, i % nt, 0)),
                  pl.BlockSpec((None, 1, D), lambda i, j: (i // nt, 0, 0)),
                  pl.BlockSpec((1, D), lambda i, j: (0, 0)),
                  pl.BlockSpec((1, D), lambda i, j: (0, 0))],
        out_specs=pl.BlockSpec((None, tm, D), lambda i, j: (i // nt, i % nt, 0)),
        out_shape=jax.ShapeDtypeStruct((bsz, s, D), F32),
        scratch_shapes=[pltpu.VMEM((D, tm), F32),
                        pltpu.VMEM((eb, tm), F32), pltpu.VMEM((eb, tm), F32),
                        pltpu.VMEM((eb, tm), BF16), pltpu.VMEM((eb, tm), BF16)],
        compiler_params=_cparams(("arbitrary", "arbitrary")),
        name="peer",
    )(ut, pu, pvt, t1, t2, tau, x1, g2, ln_w, ln_b)


def _layer(x, mod, w_in, b_in, conv_w, conv_b, norm_w, w_out, ln1_w, ln1_b,
           wq, subkeys, pu, pv, ln2_w, ln2_b):
    bsz = x.shape[0]
    sh1, sc1, g1, sh2, sc2, g2 = [m.reshape(bsz, 1, D) for m in jnp.split(mod, 6, axis=-1)]
    row = lambda v: v.reshape(1, -1)

    w_main = w_in[:, :MAIN_COLS].astype(BF16)
    w_gate = jnp.pad(w_in[:, MAIN_COLS:], ((0, 0), (0, LANE - N_GATE))).astype(BF16)
    b_gate = jnp.pad(b_in[MAIN_COLS:], (0, LANE - N_GATE))
    proj, gates = _inproj(x, sh1, sc1, w_main, row(b_in[:MAIN_COLS]), w_gate, row(b_gate))
    gates_t = jnp.swapaxes(gates[:, :, :N_GATE], 1, 2)
    hf, hb = _mlstm(proj, gates, gates_t)
    x1 = _mixout(proj, hf, hb, x, g1, conv_w, row(conv_b), row(norm_w), w_out.astype(BF16),
                 row(ln1_w), row(ln1_b))
    keys = subkeys.reshape(2 * P_HEADS, N_KEYS, N_KEYS).astype(BF16)
    ut, st = _peerq(x1, sh2, sc2, wq.astype(BF16), keys)
    t1, t2, tau = _peersel(st)
    pvt = jnp.swapaxes(pv.reshape(-1, PEER_EB, D), 1, 2).astype(BF16)
    return _peer(ut, pu.astype(BF16), pvt, t1, t2, tau, x1, g2, row(ln2_w), row(ln2_b))


def kernel(x, c, w_ada, b_ada, w_in, b_in, conv_w, conv_b, mlstm_norm_w, w_out, ln1_w, ln1_b,
           peer_wq, peer_subkeys, peer_u, peer_v, ln2_w, ln2_b):
    for l in range(w_ada.shape[0]):
        mod = _ada(c, w_ada[l], b_ada[l])
        x = _layer(x, mod, w_in[l], b_in[l], conv_w[l], conv_b[l], mlstm_norm_w[l], w_out[l],
                   ln1_w[l], ln1_b[l], peer_wq[l], peer_subkeys[l], peer_u[l], peer_v[l],
                   ln2_w[l], ln2_b[l])
    return x
```

```python
import functools

import jax
import jax.numpy as jnp
from jax import lax
from jax.experimental import pallas as pl
from jax.experimental.pallas import tpu as pltpu

D = 2048
CONV_W = 1024
CONV_K = 3
N_HEADS = 4
DQK = 128
DV = 256
N_GATE = 16
MAIN_COLS = 6144
COL_GB, COL_GC, COL_HIN, COL_Q, COL_K, COL_V, COL_O = 0, 1024, 2048, 3072, 3584, 4096, 5120
P_HEADS = 8
N_KEYS = 128
N_EXPERTS = N_KEYS * N_KEYS
TOPK = 16
LN_EPS = 1e-5
DEPTH = 1
ALPHA = (2 * DEPTH) ** 0.25

LANE = 128
SUBLANE = 8
VMEM_LIMIT = 52 * 1024 * 1024

ADA_TN = 1024
INPROJ_TM, INPROJ_TN = 1024, 1024
MLSTM_L = 128
MIX_TM = 256
PQ_TM = 256
SEL_TM = 128
PEER_TM, PEER_EB = 512, 512
MXU_WIDTH = 256
PEER_MM_TILE_N = MXU_WIDTH
PEER_MM_TILE_M = 512
PEER_MM_TILE_K1 = MXU_WIDTH
PEER_GATE_ROWS = 64
PEER_MXU_LEAD = 3

F32 = jnp.float32
BF16 = jnp.bfloat16
NEG_INF = float("-inf")


def _cparams(sem, flags=None):
    return pltpu.CompilerParams(dimension_semantics=sem, vmem_limit_bytes=VMEM_LIMIT, flags=flags)


def _ln(x):
    mu = jnp.mean(x, axis=-1, keepdims=True)
    xc = x - mu
    var = jnp.mean(xc * xc, axis=-1, keepdims=True)
    return xc * lax.rsqrt(var + LN_EPS)


def _bdot(a, b):
    return jnp.dot(a, b, preferred_element_type=F32)


def _ada_kernel(c_ref, w_ref, b_ref, o_ref):
    c = c_ref[...]
    cond = c * jax.nn.sigmoid(c)
    o_ref[...] = _bdot(cond.astype(BF16), w_ref[...].astype(BF16)) + b_ref[...]


def _ada(c, w, b):
    bsz = c.shape[0]
    n = w.shape[1]
    return pl.pallas_call(
        _ada_kernel,
        grid=(n // ADA_TN,),
        in_specs=[pl.BlockSpec((bsz, D), lambda j: (0, 0)),
                  pl.BlockSpec((D, ADA_TN), lambda j: (0, j)),
                  pl.BlockSpec((1, ADA_TN), lambda j: (0, j))],
        out_specs=pl.BlockSpec((bsz, ADA_TN), lambda j: (0, j)),
        out_shape=jax.ShapeDtypeStruct((bsz, n), F32),
        compiler_params=_cparams(("arbitrary",)),
        name="ada",
    )(c, w, b.reshape(1, n))


def _inproj_kernel(x_ref, sh_ref, sc_ref, w_ref, b_ref, wg_ref, bg_ref, proj_ref, gate_ref, u_scr):
    @pl.when(pl.program_id(2) == 0)
    def _():
        u = _ln(x_ref[...]) * (1.0 + sc_ref[...]) + sh_ref[...]
        ub = u.astype(BF16)
        u_scr[...] = ub
        gate_ref[...] = _bdot(ub, wg_ref[...]) + bg_ref[...]

    proj_ref[...] = _bdot(u_scr[...], w_ref[...]) + b_ref[...]


def _inproj(x, sh, sc, w_main, b_main, w_gate, b_gate):
    bsz, s, _ = x.shape
    tm, tn = min(INPROJ_TM, s), INPROJ_TN
    return pl.pallas_call(
        _inproj_kernel,
        grid=(bsz, s // tm, MAIN_COLS // tn),
        in_specs=[pl.BlockSpec((None, tm, D), lambda b, i, j: (b, i, 0)),
                  pl.BlockSpec((None, 1, D), lambda b, i, j: (b, 0, 0)),
                  pl.BlockSpec((None, 1, D), lambda b, i, j: (b, 0, 0)),
                  pl.BlockSpec((D, tn), lambda b, i, j: (0, j)),
                  pl.BlockSpec((1, tn), lambda b, i, j: (0, j)),
                  pl.BlockSpec((D, LANE), lambda b, i, j: (0, 0)),
                  pl.BlockSpec((1, LANE), lambda b, i, j: (0, 0))],
        out_specs=[pl.BlockSpec((None, tm, tn), lambda b, i, j: (b, i, j)),
                   pl.BlockSpec((None, tm, LANE), lambda b, i, j: (b, i, 0))],
        out_shape=[jax.ShapeDtypeStruct((bsz, s, MAIN_COLS), F32),
                   jax.ShapeDtypeStruct((bsz, s, LANE), F32)],
        scratch_shapes=[pltpu.VMEM((tm, D), BF16)],
        compiler_params=_cparams(("arbitrary", "arbitrary", "arbitrary")),
        name="inproj",
    )(x, sh, sc, w_main, b_main, w_gate, b_gate)


def _log_sigmoid(x):
    return jnp.minimum(x, 0.0) - jnp.log1p(jnp.exp(-jnp.abs(x)))


def _mlstm_chains(d, q_ref, k_ref, v_ref, gcol_ref, grow_ref, h_ref, c_scr, n_scr, m_scr):
    L = q_ref.shape[0]
    t_idx = lax.broadcasted_iota(jnp.int32, (L, L), 0)
    s_idx = lax.broadcasted_iota(jnp.int32, (L, L), 1)
    mask = (t_idx >= s_idx) if d == 0 else (t_idx <= s_idx)
    mask_f = mask.astype(F32)
    mask_tf = ((t_idx <= s_idx) if d == 0 else (t_idx >= s_idx)).astype(F32)
    gcol = gcol_ref[...]
    grow = grow_ref[...]
    hi = lax.Precision.HIGHEST
    bcols = jnp.dot(mask_f, _log_sigmoid(gcol), precision=hi, preferred_element_type=F32)
    brows = jnp.dot(_log_sigmoid(grow), mask_tf, precision=hi, preferred_element_type=F32)
    last = L - 1 if d == 0 else 0

    def chain(h):
        idx = d * N_HEADS + h
        ci, cf = d * 8 + h, d * 8 + 4 + h
        t = {}

        def weights():
            t["b_col"] = b_col = bcols[:, cf:cf + 1]
            b_row = brows[cf:cf + 1, :]
            ig_row = grow[ci:ci + 1, :]
            t["b_tot"] = brows[cf:cf + 1, last:last + 1]
            t["m_prev"] = m_prev = m_scr[idx][:, 0:1]
            dm = jnp.where(mask, b_col - b_row + ig_row, NEG_INF)
            inter = b_col + m_prev
            t["m_t"] = m_t = jnp.maximum(inter, jnp.max(dm, axis=1, keepdims=True))
            t["w_intra"] = jnp.exp(dm - m_t)
            t["w_inter"] = jnp.exp(inter - m_t)

        def scores():
            t["q"] = q = q_ref[:, h * DQK:(h + 1) * DQK] * (DQK ** -0.5)
            t["qb"] = qb = q.astype(BF16)
            t["kb"] = kb = k_ref[:, h * DQK:(h + 1) * DQK].astype(BF16)
            qk = lax.dot_general(qb, kb, (((1,), (1,)), ((), ())), preferred_element_type=F32)
            t["sw"] = qk * t.pop("w_intra")

        def output():
            sw, w_inter, q = t.pop("sw"), t.pop("w_inter"), t.pop("q")
            v = v_ref[:, h * DV:(h + 1) * DV]
            num = (_bdot(sw.astype(BF16), v.astype(BF16))
                   + w_inter * _bdot(t.pop("qb"), c_scr[idx].astype(BF16)))
            den = (jnp.sum(sw, axis=1, keepdims=True)
                   + w_inter * jnp.sum(q * n_scr[idx], axis=1, keepdims=True))
            h_ref[:, h * DV:(h + 1) * DV] = num / jnp.maximum(jnp.abs(den), jnp.exp(-t.pop("m_t")))

        def state():
            b_tot, m_prev = t.pop("b_tot"), t.pop("m_prev")
            a_col = b_tot - t.pop("b_col") + gcol[:, ci:ci + 1]
            m_new = jnp.maximum(b_tot + m_prev, jnp.max(a_col, axis=0, keepdims=True))
            w_a = jnp.exp(a_col - m_new)
            decay = jnp.exp(b_tot + m_prev - m_new)
            k = k_ref[:, h * DQK:(h + 1) * DQK]
            v = v_ref[:, h * DV:(h + 1) * DV]
            kv = lax.dot_general(t.pop("kb"), (w_a * v).astype(BF16), (((0,), (0,)), ((), ())),
                                 preferred_element_type=F32)
            c_scr[idx] = decay * c_scr[idx] + kv
            n_scr[idx] = decay * n_scr[idx] + jnp.sum(w_a * k, axis=0, keepdims=True)
            m_scr[idx] = jnp.broadcast_to(m_new, (1, LANE))

        return [weights, scores, output, state]

    return [chain(h) for h in range(N_HEADS)]


def _mlstm_kernel(qf, kf, vf, gcf, grf, qb, kb, vb, gcb, grb, hf_ref, hb_ref, c_scr, n_scr, m_scr):
    @pl.when(pl.program_id(1) == 0)
    def _():
        c_scr[...] = jnp.zeros_like(c_scr)
        n_scr[...] = jnp.zeros_like(n_scr)
        m_scr[...] = jnp.zeros_like(m_scr)

    chains = (_mlstm_chains(0, qf, kf, vf, gcf, grf, hf_ref, c_scr, n_scr, m_scr)
              + _mlstm_chains(1, qb, kb, vb, gcb, grb, hb_ref, c_scr, n_scr, m_scr))
    for chain in chains:
        for step in chain:
            step()


def _mlstm(proj, gates, gates_t):
    bsz, s, _ = proj.shape
    L = min(MLSTM_L, s)
    nc = s // L
    qkw, vw = N_HEADS * DQK, N_HEADS * DV

    def specs(row):
        return [pl.BlockSpec((None, L, qkw), lambda b, j: (b, row(j), COL_Q // qkw)),
                pl.BlockSpec((None, L, qkw), lambda b, j: (b, row(j), COL_K // qkw)),
                pl.BlockSpec((None, L, vw), lambda b, j: (b, row(j), COL_V // vw)),
                pl.BlockSpec((None, L, LANE), lambda b, j: (b, row(j), 0)),
                pl.BlockSpec((None, N_GATE, L), lambda b, j: (b, 0, row(j)))]

    fwd = lambda j: j
    bwd = lambda j: nc - 1 - j
    return pl.pallas_call(
        _mlstm_kernel,
        grid=(bsz, nc),
        in_specs=specs(fwd) + specs(bwd),
        out_specs=[pl.BlockSpec((None, L, vw), lambda b, j: (b, j, 0)),
                   pl.BlockSpec((None, L, vw), lambda b, j: (b, nc - 1 - j, 0))],
        out_shape=[jax.ShapeDtypeStruct((bsz, s, vw), F32)] * 2,
        scratch_shapes=[pltpu.VMEM((2 * N_HEADS, DQK, DV), F32),
                        pltpu.VMEM((2 * N_HEADS, 1, DQK), F32),
                        pltpu.VMEM((2 * N_HEADS, 1, LANE), F32)],
        compiler_params=_cparams(("arbitrary", "arbitrary")),
        name="mlstm",
    )(proj, proj, proj, gates, gates_t, proj, proj, proj, gates, gates_t)


def _mixout_kernel(gb_ref, gc_ref, hin_ref, gcp_ref, hinp_ref, gcn_ref, hinn_ref, hf_ref, hb_ref, o_ref,
                   x_ref, g1_ref, cw_ref, cb_ref, nw_ref, wout_ref, lw_ref, lb_ref, x1_ref, y_scr):
    i = pl.program_id(1)
    tm = x_ref.shape[0]
    z = gc_ref[...] * hin_ref[...]
    z_before = (gcp_ref[...] * hinp_ref[...])[SUBLANE - 1:SUBLANE, :]
    z_after = (gcn_ref[...] * hinn_ref[...])[0:1, :]
    z_before = jnp.where(i == 0, 0.0, z_before)
    z_after = jnp.where(i == pl.num_programs(1) - 1, 0.0, z_after)
    row = lax.broadcasted_iota(jnp.int32, z.shape, 0)
    z_prev = jnp.where(row == 0, z_before, pltpu.roll(z, 1, axis=0))
    z_next = jnp.where(row == tm - 1, z_after, pltpu.roll(z, tm - 1, axis=0))
    cw = cw_ref[...]
    conv = cw[0:1, :] * z_prev + cw[1:2, :] * z + cw[2:3, :] * z_next + cb_ref[...]
    y_scr[:, 0:CONV_W] = (gb_ref[...] * conv).astype(BF16)
    for h in range(N_HEADS):
        sl = slice(h * DV, (h + 1) * DV)
        hn = _ln(hf_ref[:, sl] + hb_ref[:, sl]) * nw_ref[:, sl]
        y_scr[:, CONV_W + h * DV:CONV_W + (h + 1) * DV] = (jax.nn.sigmoid(o_ref[:, sl]) * hn).astype(BF16)
    y = _bdot(y_scr[...], wout_ref[...])
    x1_ref[...] = _ln(ALPHA * x_ref[...] + g1_ref[...] * y) * lw_ref[...] + lb_ref[...]


def _mixout(proj, hf, hb, x, g1, conv_w, conv_b, norm_w, w_out, ln_w, ln_b):
    bsz, s, _ = x.shape
    tm = min(MIX_TM, s)
    nt = s // tm
    rb = tm // SUBLANE
    last_rb = s // SUBLANE - 1

    def col(width, start):
        return pl.BlockSpec((None, tm, width), lambda b, i: (b, i, start // width))

    def halo_prev(start):
        return pl.BlockSpec((None, SUBLANE, CONV_W),
                            lambda b, i: (b, jnp.maximum(i * rb - 1, 0), start // CONV_W))

    def halo_next(start):
        return pl.BlockSpec((None, SUBLANE, CONV_W),
                            lambda b, i: (b, jnp.minimum((i + 1) * rb, last_rb), start // CONV_W))

    full = lambda shape: pl.BlockSpec(shape, lambda b, i: tuple(0 for _ in shape))
    return pl.pallas_call(
        _mixout_kernel,
        grid=(bsz, nt),
        in_specs=[col(CONV_W, COL_GB), col(CONV_W, COL_GC), col(CONV_W, COL_HIN),
                  halo_prev(COL_GC), halo_prev(COL_HIN), halo_next(COL_GC), halo_next(COL_HIN),
                  pl.BlockSpec((None, tm, N_HEADS * DV), lambda b, i: (b, i, 0)),
                  pl.BlockSpec((None, tm, N_HEADS * DV), lambda b, i: (b, i, 0)),
                  col(N_HEADS * DV, COL_O),
                  pl.BlockSpec((None, tm, D), lambda b, i: (b, i, 0)),
                  pl.BlockSpec((None, 1, D), lambda b, i: (b, 0, 0)),
                  full((CONV_K, CONV_W)), full((1, CONV_W)), full((1, N_HEADS * DV)),
                  full((D, D)), full((1, D)), full((1, D))],
        out_specs=pl.BlockSpec((None, tm, D), lambda b, i: (b, i, 0)),
        out_shape=jax.ShapeDtypeStruct((bsz, s, D), F32),
        scratch_shapes=[pltpu.VMEM((tm, D), BF16)],
        compiler_params=_cparams(("arbitrary", "arbitrary")),
        name="mixout",
    )(proj, proj, proj, proj, proj, proj, proj, hf, hb, proj, x, g1,
      conv_w, conv_b, norm_w, w_out, ln_w, ln_b)


def _peerq_kernel(x1_ref, sh_ref, sc_ref, wq_ref, keys_ref, ut_ref, st_ref):
    u = _ln(x1_ref[...]) * (1.0 + sc_ref[...]) + sh_ref[...]
    ut_ref[...] = u.T.astype(BF16)
    qh = _bdot(u.astype(BF16), wq_ref[...]).astype(BF16)
    for hp in range(2 * P_HEADS):
        st_ref[hp] = lax.dot_general(keys_ref[hp], qh[:, hp * N_KEYS:(hp + 1) * N_KEYS],
                                     (((1,), (1,)), ((), ())), preferred_element_type=F32)


def _peerq(x1, sh, sc, wq, keys):
    bsz, s, _ = x1.shape
    tm = min(PQ_TM, s)
    nt = s // tm
    t = bsz * s
    return pl.pallas_call(
        _peerq_kernel,
        grid=(bsz, nt),
        in_specs=[pl.BlockSpec((None, tm, D), lambda b, i: (b, i, 0)),
                  pl.BlockSpec((None, 1, D), lambda b, i: (b, 0, 0)),
                  pl.BlockSpec((None, 1, D), lambda b, i: (b, 0, 0)),
                  pl.BlockSpec((D, D), lambda b, i: (0, 0)),
                  pl.BlockSpec((2 * P_HEADS, N_KEYS, N_KEYS), lambda b, i: (0, 0, 0))],
        out_specs=[pl.BlockSpec((D, tm), lambda b, i: (0, b * nt + i)),
                   pl.BlockSpec((2 * P_HEADS, N_KEYS, tm), lambda b, i: (0, 0, b * nt + i))],
        out_shape=[jax.ShapeDtypeStruct((D, t), BF16),
                   jax.ShapeDtypeStruct((2 * P_HEADS, N_KEYS, t), F32)],
        compiler_params=_cparams(("arbitrary", "arbitrary")),
        name="peerq",
    )(x1, sh, sc, wq, keys)


def _sort_network(n):
    pairs = []
    p = 1
    while p < n:
        k = p
        while k >= 1:
            for j in range(k % p, n - k, 2 * k):
                for i in range(min(k, n - j - k)):
                    if (i + j) // (2 * p) == (i + j + k) // (2 * p):
                        pairs.append((i + j, i + j + k))
            k //= 2
        p *= 2
    return pairs


def _exchange(v, i, j):
    if v[j] is None:
        return
    if v[i] is None:
        v[i], v[j] = v[j], None
        return
    v[i], v[j] = jnp.maximum(v[i], v[j]), jnp.minimum(v[i], v[j])


def _top_sorted(v):
    v = list(v)
    for i, j in _sort_network(TOPK):
        _exchange(v, i, j)
    shift = SUBLANE // 2
    while shift >= 1:
        other = [None if x is None else pltpu.roll(x, shift, axis=0) for x in v]
        merged = []
        for r in range(TOPK):
            x, y = v[r], other[TOPK - 1 - r]
            merged.append(y if x is None else x if y is None else jnp.maximum(x, y))
        v = merged
        d = TOPK // 2
        while d >= 1:
            for i in range(TOPK):
                if i & d == 0:
                    _exchange(v, i, i + d)
            d //= 2
        shift //= 2
    return v


def _pack_rows(v):
    sub = lax.broadcasted_iota(jnp.int32, v[0].shape, 0)
    groups = []
    for g in range(0, len(v), SUBLANE):
        x = v[g]
        for r in range(1, SUBLANE):
            x = jnp.where(sub == r, v[g + r], x)
        groups.append(x)
    return groups


def _candidate_sums(a, b):
    b_lo, b_hi = _pack_rows(b)
    a_hi = _pack_rows(a)[1]
    parts = [a[0] + b_lo, a[0] + b_hi]
    parts += [a[i] + b_lo for i in range(1, SUBLANE)]
    parts.append(a_hi + b[0])
    return parts + [None] * (TOPK - len(parts))


def _peersel_kernel(st_ref, t1_ref, t2_ref, tau_ref):
    log2e = 1.4426950408889634

    def head(h, carry):
        s1 = st_ref[2 * h]
        s2 = st_ref[2 * h + 1]
        split = lambda x: [x[r * SUBLANE:(r + 1) * SUBLANE, :] for r in range(N_KEYS // SUBLANE)]
        a = _top_sorted(split(s1))
        b = _top_sorted(split(s2))
        a0, b0 = a[0][0:1, :], b[0][0:1, :]
        a_sh = [x - a0 for x in a]
        b_sh = [x - b0 for x in b]
        best = _top_sorted(_candidate_sums(a_sh, b_sh))
        z = jnp.exp(best[0])
        for x in best[1:]:
            z = z + jnp.exp(x)
        log_z = jnp.log(z[0:1, :])
        t1_ref[h] = ((s1 - a0) - log_z) * log2e
        t2_ref[h] = (s2 - b0) * log2e
        best_t = _top_sorted(_candidate_sums([(x - log_z) * log2e for x in a_sh],
                                             [x * log2e for x in b_sh]))
        tau_ref[pl.ds(h, 1), :] = best_t[TOPK - 1][0:1, :]
        return carry

    lax.fori_loop(0, P_HEADS, head, 0)


def _peersel(st):
    t = st.shape[2]
    tm = SEL_TM
    return pl.pallas_call(
        _peersel_kernel,
        grid=(t // tm,),
        in_specs=[pl.BlockSpec((2 * P_HEADS, N_KEYS, tm), lambda i: (0, 0, i))],
        out_specs=[pl.BlockSpec((P_HEADS, N_KEYS, tm), lambda i: (0, 0, i)),
                   pl.BlockSpec((P_HEADS, N_KEYS, tm), lambda i: (0, 0, i)),
                   pl.BlockSpec((P_HEADS, tm), lambda i: (0, i))],
        out_shape=[jax.ShapeDtypeStruct((P_HEADS, N_KEYS, t), F32),
                   jax.ShapeDtypeStruct((P_HEADS, N_KEYS, t), F32),
                   jax.ShapeDtypeStruct((P_HEADS, t), F32)],
        compiler_params=_cparams(("arbitrary",)),
        name="peersel",
    )(st)


def _gelu(x):
    return 0.5 * x * (1.0 + lax.erf(x * (2.0 ** -0.5)))


def _peer_gate_tiles(block, h_ref, a_ref, t1_ref, t2_ref, tau_ref):
    eb, tm = h_ref.shape
    keys_per_step = eb // N_KEYS
    first_key = block * keys_per_step
    group = pl.multiple_of((first_key // SUBLANE) * SUBLANE, SUBLANE)
    offset = first_key - group

    def tile(kk, part, c):
        keys2 = slice(part * PEER_GATE_ROWS, (part + 1) * PEER_GATE_ROWS)
        rows = slice(kk * N_KEYS + keys2.start, kk * N_KEYS + keys2.stop)
        cols = slice(c * LANE, (c + 1) * LANE)
        w = jnp.zeros((PEER_GATE_ROWS, LANE), F32)
        for h in range(P_HEADS):
            t1_group = t1_ref[h, pl.ds(group, SUBLANE), cols]
            t1_row = t1_group[kk:kk + 1, :]
            for g in range(1, SUBLANE // keys_per_step):
                r = g * keys_per_step + kk
                t1_row = jnp.where(offset == g * keys_per_step, t1_group[r:r + 1, :], t1_row)
            s = t1_row + t2_ref[h, keys2, cols]
            w = w + jnp.where(s >= tau_ref[h:h + 1, cols], jnp.exp2(s), 0.0)
        a_ref[rows, cols] = (_gelu(h_ref[rows, cols]) * w).astype(BF16)

    return [functools.partial(tile, kk, part, c) for kk in range(keys_per_step)
            for part in range(N_KEYS // PEER_GATE_ROWS) for c in range(tm // LANE)]


def _matmul_pieces(out_ref, lhs_ref, rhs_ref, tile_m, tile_n, tile_k, accumulate):
    def piece(m, n, k):
        rows = slice(m * tile_m, (m + 1) * tile_m)
        cols = slice(n * tile_n, (n + 1) * tile_n)
        depth = slice(k * tile_k, (k + 1) * tile_k)
        res = _bdot(lhs_ref[rows, depth], rhs_ref[depth, cols])
        out_ref[rows, cols] = out_ref[rows, cols] + res if (accumulate or k > 0) else res

    return [functools.partial(piece, m, n, k)
            for n in range(out_ref.shape[1] // tile_n) for m in range(out_ref.shape[0] // tile_m)
            for k in range(lhs_ref.shape[1] // tile_k)]


def _peer_kernel(ut_ref, pu_ref, pvt_ref, t1_ref, t2_ref, tau_ref, x1_ref, g2_ref, lw_ref, lb_ref,
                 out_ref, acc_scr, h0_scr, h1_scr, a0_scr, a1_scr):
    s = pl.program_id(1)
    last = pl.num_programs(1) - 1
    n_blocks = last - 1

    @pl.when(s == 0)
    def _():
        acc_scr[...] = jnp.zeros_like(acc_scr)
        a0_scr[...] = jnp.zeros_like(a0_scr)
        h0_scr[...] = _bdot(pu_ref[...], ut_ref[...])

    def stages(h_write, h_read, a_write, a_read):
        block = jnp.minimum(s - 1, n_blocks - 1)
        gate = _peer_gate_tiles(block, h_read, a_write, t1_ref, t2_ref, tau_ref)
        mxu = (_matmul_pieces(h_write, pu_ref, ut_ref, PEER_MM_TILE_M, PEER_MM_TILE_N, PEER_MM_TILE_K1, False)
               + _matmul_pieces(acc_scr, pvt_ref, a_read, PEER_MM_TILE_M, PEER_MM_TILE_N, a_read.shape[0], True))
        issued = 0
        for i, gate_tile in enumerate(gate):
            ahead = min(len(mxu), -(-(i + 1 + PEER_MXU_LEAD) * len(mxu) // len(gate)))
            for piece in mxu[issued:ahead]:
                piece()
            issued = ahead
            gate_tile()

    @pl.when((s % 2 == 0) & (s > 0))
    def _():
        stages(h0_scr, h1_scr, a0_scr, a1_scr)

    @pl.when((s % 2 == 1) & (s < last))
    def _():
        stages(h1_scr, h0_scr, a1_scr, a0_scr)

    @pl.when(s == last)
    def _():
        acc_scr[...] += _bdot(pvt_ref[...], a0_scr[...])
        y = acc_scr[...].T
        out_ref[...] = _ln(ALPHA * x1_ref[...] + g2_ref[...] * y) * lw_ref[...] + lb_ref[...]


def _peer(ut, pu, pvt, t1, t2, tau, x1, g2, ln_w, ln_b):
    bsz, s, _ = x1.shape
    tm = min(PEER_TM, s)
    nt = s // tm
    eb = PEER_EB
    nb = N_EXPERTS // eb
    assert nb % 2 == 0
    return pl.pallas_call(
        _peer_kernel,
        grid=(bsz * nt, nb + 2),
        in_specs=[pl.BlockSpec((D, tm), lambda i, j: (0, i)),
                  pl.BlockSpec((eb, D), lambda i, j: (jnp.minimum(j, nb - 1), 0)),
                  pl.BlockSpec((None, D, eb), lambda i, j: (jnp.clip(j - 2, 0, nb - 1), 0, 0)),
                  pl.BlockSpec((P_HEADS, N_KEYS, tm), lambda i, j: (0, 0, i)),
                  pl.BlockSpec((P_HEADS, N_KEYS, tm), lambda i, j: (0, 0, i)),
                  pl.BlockSpec((P_HEADS, tm), lambda i, j: (0, i)),
                  pl.BlockSpec((None, tm, D), lambda i, j: (i // nt, i % nt, 0)),
                  pl.BlockSpec((None, 1, D), lambda i, j: (i // nt, 0, 0)),
                  pl.BlockSpec((1, D), lambda i, j: (0, 0)),
                  pl.BlockSpec((1, D), lambda i, j: (0, 0))],
        out_specs=pl.BlockSpec((None, tm, D), lambda i, j: (i // nt, i % nt, 0)),
        out_shape=jax.ShapeDtypeStruct((bsz, s, D), F32),
        scratch_shapes=[pltpu.VMEM((D, tm), F32),
                        pltpu.VMEM((eb, tm), F32), pltpu.VMEM((eb, tm), F32),
                        pltpu.VMEM((eb, tm), BF16), pltpu.VMEM((eb, tm), BF16)],
        compiler_params=_cparams(("arbitrary", "arbitrary")),
        name="peer",
    )(ut, pu, pvt, t1, t2, tau, x1, g2, ln_w, ln_b)


def _layer(x, mod, w_in, b_in, conv_w, conv_b, norm_w, w_out, ln1_w, ln1_b,
           wq, subkeys, pu, pv, ln2_w, ln2_b):
    bsz = x.shape[0]
    sh1, sc1, g1, sh2, sc2, g2 = [m.reshape(bsz, 1, D) for m in jnp.split(mod, 6, axis=-1)]
    row = lambda v: v.reshape(1, -1)

    w_main = w_in[:, :MAIN_COLS].astype(BF16)
    w_gate = jnp.pad(w_in[:, MAIN_COLS:], ((0, 0), (0, LANE - N_GATE))).astype(BF16)
    b_gate = jnp.pad(b_in[MAIN_COLS:], (0, LANE - N_GATE))
    proj, gates = _inproj(x, sh1, sc1, w_main, row(b_in[:MAIN_COLS]), w_gate, row(b_gate))
    gates_t = jnp.swapaxes(gates[:, :, :N_GATE], 1, 2)
    hf, hb = _mlstm(proj, gates, gates_t)
    x1 = _mixout(proj, hf, hb, x, g1, conv_w, row(conv_b), row(norm_w), w_out.astype(BF16),
                 row(ln1_w), row(ln1_b))
    keys = subkeys.reshape(2 * P_HEADS, N_KEYS, N_KEYS).astype(BF16)
    ut, st = _peerq(x1, sh2, sc2, wq.astype(BF16), keys)
    t1, t2, tau = _peersel(st)
    pvt = jnp.swapaxes(pv.reshape(-1, PEER_EB, D), 1, 2).astype(BF16)
    return _peer(ut, pu.astype(BF16), pvt, t1, t2, tau, x1, g2, row(ln2_w), row(ln2_b))


def kernel(x, c, w_ada, b_ada, w_in, b_in, conv_w, conv_b, mlstm_norm_w, w_out, ln1_w, ln1_b,
           peer_wq, peer_subkeys, peer_u, peer_v, ln2_w, ln2_b):
    for l in range(w_ada.shape[0]):
        mod = _ada(c, w_ada[l], b_ada[l])
        x = _layer(x, mod, w_in[l], b_in[l], conv_w[l], conv_b[l], mlstm_norm_w[l], w_out[l],
                   ln1_w[l], ln1_b[l], peer_wq[l], peer_subkeys[l], peer_u[l], peer_v[l],
                   ln2_w[l], ln2_b[l])
    return x
```

```python
import functools

import jax
import jax.numpy as jnp
from jax import lax
from jax.experimental import pallas as pl
from jax.experimental.pallas import tpu as pltpu

D = 2048
CONV_W = 1024
CONV_K = 3
N_HEADS = 4
DQK = 128
DV = 256
N_GATE = 16
MAIN_COLS = 6144
COL_GB, COL_GC, COL_HIN, COL_Q, COL_K, COL_V, COL_O = 0, 1024, 2048, 3072, 3584, 4096, 5120
P_HEADS = 8
N_KEYS = 128
N_EXPERTS = N_KEYS * N_KEYS
TOPK = 16
LN_EPS = 1e-5
DEPTH = 1
ALPHA = (2 * DEPTH) ** 0.25

LANE = 128
SUBLANE = 8
VMEM_LIMIT = 52 * 1024 * 1024

ADA_TN = 1024
INPROJ_TM, INPROJ_TN = 1024, 1024
MLSTM_L = 256
MIX_TM = 256
PQ_TM = 256
SEL_TM = 128
PEER_TM, PEER_EB = 512, 512
MXU_WIDTH = 256
PEER_MM_TILE_N = MXU_WIDTH
PEER_MM_TILE_M = 512
PEER_MM_TILE_K1 = MXU_WIDTH
PEER_GATE_ROWS = 64
PEER_MXU_LEAD = 1

F32 = jnp.float32
BF16 = jnp.bfloat16
NEG_INF = float("-inf")


def _cparams(sem, flags=None):
    return pltpu.CompilerParams(dimension_semantics=sem, vmem_limit_bytes=VMEM_LIMIT, flags=flags)


def _ln(x):
    mu = jnp.mean(x, axis=-1, keepdims=True)
    xc = x - mu
    var = jnp.mean(xc * xc, axis=-1, keepdims=True)
    return xc * lax.rsqrt(var + LN_EPS)


def _bdot(a, b):
    return jnp.dot(a, b, preferred_element_type=F32)


def _ada_kernel(c_ref, w_ref, b_ref, o_ref):
    c = c_ref[...]
    cond = c * jax.nn.sigmoid(c)
    o_ref[...] = _bdot(cond.astype(BF16), w_ref[...].astype(BF16)) + b_ref[...]


def _ada(c, w, b):
    bsz = c.shape[0]
    n = w.shape[1]
    return pl.pallas_call(
        _ada_kernel,
        grid=(n // ADA_TN,),
        in_specs=[pl.BlockSpec((bsz, D), lambda j: (0, 0)),
                  pl.BlockSpec((D, ADA_TN), lambda j: (0, j)),
                  pl.BlockSpec((1, ADA_TN), lambda j: (0, j))],
        out_specs=pl.BlockSpec((bsz, ADA_TN), lambda j: (0, j)),
        out_shape=jax.ShapeDtypeStruct((bsz, n), F32),
        compiler_params=_cparams(("arbitrary",)),
        name="ada",
    )(c, w, b.reshape(1, n))


def _inproj_kernel(x_ref, sh_ref, sc_ref, w_ref, b_ref, wg_ref, bg_ref, proj_ref, gate_ref, u_scr):
    @pl.when(pl.program_id(2) == 0)
    def _():
        u = _ln(x_ref[...]) * (1.0 + sc_ref[...]) + sh_ref[...]
        ub = u.astype(BF16)
        u_scr[...] = ub
        gate_ref[...] = _bdot(ub, wg_ref[...]) + bg_ref[...]

    proj_ref[...] = _bdot(u_scr[...], w_ref[...]) + b_ref[...]


def _inproj(x, sh, sc, w_main, b_main, w_gate, b_gate):
    bsz, s, _ = x.shape
    tm, tn = min(INPROJ_TM, s), INPROJ_TN
    return pl.pallas_call(
        _inproj_kernel,
        grid=(bsz, s // tm, MAIN_COLS // tn),
        in_specs=[pl.BlockSpec((None, tm, D), lambda b, i, j: (b, i, 0)),
                  pl.BlockSpec((None, 1, D), lambda b, i, j: (b, 0, 0)),
                  pl.BlockSpec((None, 1, D), lambda b, i, j: (b, 0, 0)),
                  pl.BlockSpec((D, tn), lambda b, i, j: (0, j)),
                  pl.BlockSpec((1, tn), lambda b, i, j: (0, j)),
                  pl.BlockSpec((D, LANE), lambda b, i, j: (0, 0)),
                  pl.BlockSpec((1, LANE), lambda b, i, j: (0, 0))],
        out_specs=[pl.BlockSpec((None, tm, tn), lambda b, i, j: (b, i, j)),
                   pl.BlockSpec((None, tm, LANE), lambda b, i, j: (b, i, 0))],
        out_shape=[jax.ShapeDtypeStruct((bsz, s, MAIN_COLS), F32),
                   jax.ShapeDtypeStruct((bsz, s, LANE), F32)],
        scratch_shapes=[pltpu.VMEM((tm, D), BF16)],
        compiler_params=_cparams(("arbitrary", "arbitrary", "arbitrary")),
        name="inproj",
    )(x, sh, sc, w_main, b_main, w_gate, b_gate)


def _log_sigmoid(x):
    return jnp.minimum(x, 0.0) - jnp.log1p(jnp.exp(-jnp.abs(x)))


def _mlstm_chains(d, q_ref, k_ref, v_ref, gcol_ref, grow_ref, h_ref, c_scr, n_scr, m_scr):
    L = q_ref.shape[0]
    t_idx = lax.broadcasted_iota(jnp.int32, (L, L), 0)
    s_idx = lax.broadcasted_iota(jnp.int32, (L, L), 1)
    mask = (t_idx >= s_idx) if d == 0 else (t_idx <= s_idx)
    mask_f = mask.astype(F32)
    mask_tf = ((t_idx <= s_idx) if d == 0 else (t_idx >= s_idx)).astype(F32)
    gcol = gcol_ref[...]
    grow = grow_ref[...]
    hi = lax.Precision.HIGHEST
    bcols = jnp.dot(mask_f, _log_sigmoid(gcol), precision=hi, preferred_element_type=F32)
    brows = jnp.dot(_log_sigmoid(grow), mask_tf, precision=hi, preferred_element_type=F32)
    last = L - 1 if d == 0 else 0

    def chain(h):
        idx = d * N_HEADS + h
        ci, cf = d * 8 + h, d * 8 + 4 + h
        t = {}

        def weights():
            t["b_col"] = b_col = bcols[:, cf:cf + 1]
            b_row = brows[cf:cf + 1, :]
            ig_row = grow[ci:ci + 1, :]
            t["b_tot"] = brows[cf:cf + 1, last:last + 1]
            t["m_prev"] = m_prev = m_scr[idx][:, 0:1]
            dm = jnp.where(mask, b_col - b_row + ig_row, NEG_INF)
            inter = b_col + m_prev
            t["m_t"] = m_t = jnp.maximum(inter, jnp.max(dm, axis=1, keepdims=True))
            t["w_intra"] = jnp.exp(dm - m_t)
            t["w_inter"] = jnp.exp(inter - m_t)

        def scores():
            t["q"] = q = q_ref[:, h * DQK:(h + 1) * DQK] * (DQK ** -0.5)
            t["qb"] = qb = q.astype(BF16)
            t["kb"] = kb = k_ref[:, h * DQK:(h + 1) * DQK].astype(BF16)
            qk = lax.dot_general(qb, kb, (((1,), (1,)), ((), ())), preferred_element_type=F32)
            t["sw"] = qk * t.pop("w_intra")

        def output():
            sw, w_inter, q = t.pop("sw"), t.pop("w_inter"), t.pop("q")
            v = v_ref[:, h * DV:(h + 1) * DV]
            num = (_bdot(sw.astype(BF16), v.astype(BF16))
                   + w_inter * _bdot(t.pop("qb"), c_scr[idx].astype(BF16)))
            den = (jnp.sum(sw, axis=1, keepdims=True)
                   + w_inter * jnp.sum(q * n_scr[idx], axis=1, keepdims=True))
            h_ref[:, h * DV:(h + 1) * DV] = num / jnp.maximum(jnp.abs(den), jnp.exp(-t.pop("m_t")))

        def state():
            b_tot, m_prev = t.pop("b_tot"), t.pop("m_prev")
            a_col = b_tot - t.pop("b_col") + gcol[:, ci:ci + 1]
            m_new = jnp.maximum(b_tot + m_prev, jnp.max(a_col, axis=0, keepdims=True))
            w_a = jnp.exp(a_col - m_new)
            decay = jnp.exp(b_tot + m_prev - m_new)
            k = k_ref[:, h * DQK:(h + 1) * DQK]
            v = v_ref[:, h * DV:(h + 1) * DV]
            kv = lax.dot_general(t.pop("kb"), (w_a * v).astype(BF16), (((0,), (0,)), ((), ())),
                                 preferred_element_type=F32)
            c_scr[idx] = decay * c_scr[idx] + kv
            n_scr[idx] = decay * n_scr[idx] + jnp.sum(w_a * k, axis=0, keepdims=True)
            m_scr[idx] = jnp.broadcast_to(m_new, (1, LANE))

        return [weights, scores, output, state]

    return [chain(h) for h in range(N_HEADS)]


def _mlstm_kernel(qf, kf, vf, gcf, grf, qb, kb, vb, gcb, grb, hf_ref, hb_ref, c_scr, n_scr, m_scr):
    @pl.when(pl.program_id(1) == 0)
    def _():
        c_scr[...] = jnp.zeros_like(c_scr)
        n_scr[...] = jnp.zeros_like(n_scr)
        m_scr[...] = jnp.zeros_like(m_scr)

    chains = (_mlstm_chains(0, qf, kf, vf, gcf, grf, hf_ref, c_scr, n_scr, m_scr)
              + _mlstm_chains(1, qb, kb, vb, gcb, grb, hb_ref, c_scr, n_scr, m_scr))
    for chain in chains:
        for step in chain:
            step()


def _mlstm(proj, gates, gates_t):
    bsz, s, _ = proj.shape
    L = min(MLSTM_L, s)
    nc = s // L
    qkw, vw = N_HEADS * DQK, N_HEADS * DV

    def specs(row):
        return [pl.BlockSpec((None, L, qkw), lambda b, j: (b, row(j), COL_Q // qkw)),
                pl.BlockSpec((None, L, qkw), lambda b, j: (b, row(j), COL_K // qkw)),
                pl.BlockSpec((None, L, vw), lambda b, j: (b, row(j), COL_V // vw)),
                pl.BlockSpec((None, L, LANE), lambda b, j: (b, row(j), 0)),
                pl.BlockSpec((None, N_GATE, L), lambda b, j: (b, 0, row(j)))]

    fwd = lambda j: j
    bwd = lambda j: nc - 1 - j
    return pl.pallas_call(
        _mlstm_kernel,
        grid=(bsz, nc),
        in_specs=specs(fwd) + specs(bwd),
        out_specs=[pl.BlockSpec((None, L, vw), lambda b, j: (b, j, 0)),
                   pl.BlockSpec((None, L, vw), lambda b, j: (b, nc - 1 - j, 0))],
        out_shape=[jax.ShapeDtypeStruct((bsz, s, vw), F32)] * 2,
        scratch_shapes=[pltpu.VMEM((2 * N_HEADS, DQK, DV), F32),
                        pltpu.VMEM((2 * N_HEADS, 1, DQK), F32),
                        pltpu.VMEM((2 * N_HEADS, 1, LANE), F32)],
        compiler_params=_cparams(("arbitrary", "arbitrary")),
        name="mlstm",
    )(proj, proj, proj, gates, gates_t, proj, proj, proj, gates, gates_t)


def _mixout_kernel(gb_ref, gc_ref, hin_ref, gcp_ref, hinp_ref, gcn_ref, hinn_ref, hf_ref, hb_ref, o_ref,
                   x_ref, g1_ref, cw_ref, cb_ref, nw_ref, wout_ref, lw_ref, lb_ref, x1_ref, y_scr):
    i = pl.program_id(1)
    tm = x_ref.shape[0]
    z = gc_ref[...] * hin_ref[...]
    z_before = (gcp_ref[...] * hinp_ref[...])[SUBLANE - 1:SUBLANE, :]
    z_after = (gcn_ref[...] * hinn_ref[...])[0:1, :]
    z_before = jnp.where(i == 0, 0.0, z_before)
    z_after = jnp.where(i == pl.num_programs(1) - 1, 0.0, z_after)
    row = lax.broadcasted_iota(jnp.int32, z.shape, 0)
    z_prev = jnp.where(row == 0, z_before, pltpu.roll(z, 1, axis=0))
    z_next = jnp.where(row == tm - 1, z_after, pltpu.roll(z, tm - 1, axis=0))
    cw = cw_ref[...]
    conv = cw[0:1, :] * z_prev + cw[1:2, :] * z + cw[2:3, :] * z_next + cb_ref[...]
    y_scr[:, 0:CONV_W] = (gb_ref[...] * conv).astype(BF16)
    for h in range(N_HEADS):
        sl = slice(h * DV, (h + 1) * DV)
        hn = _ln(hf_ref[:, sl] + hb_ref[:, sl]) * nw_ref[:, sl]
        y_scr[:, CONV_W + h * DV:CONV_W + (h + 1) * DV] = (jax.nn.sigmoid(o_ref[:, sl]) * hn).astype(BF16)
    y = _bdot(y_scr[...], wout_ref[...])
    x1_ref[...] = _ln(ALPHA * x_ref[...] + g1_ref[...] * y) * lw_ref[...] + lb_ref[...]


def _mixout(proj, hf, hb, x, g1, conv_w, conv_b, norm_w, w_out, ln_w, ln_b):
    bsz, s, _ = x.shape
    tm = min(MIX_TM, s)
    nt = s // tm
    rb = tm // SUBLANE
    last_rb = s // SUBLANE - 1

    def col(width, start):
        return pl.BlockSpec((None, tm, width), lambda b, i: (b, i, start // width))

    def halo_prev(start):
        return pl.BlockSpec((None, SUBLANE, CONV_W),
                            lambda b, i: (b, jnp.maximum(i * rb - 1, 0), start // CONV_W))

    def halo_next(start):
        return pl.BlockSpec((None, SUBLANE, CONV_W),
                            lambda b, i: (b, jnp.minimum((i + 1) * rb, last_rb), start // CONV_W))

    full = lambda shape: pl.BlockSpec(shape, lambda b, i: tuple(0 for _ in shape))
    return pl.pallas_call(
        _mixout_kernel,
        grid=(bsz, nt),
        in_specs=[col(CONV_W, COL_GB), col(CONV_W, COL_GC), col(CONV_W, COL_HIN),
                  halo_prev(COL_GC), halo_prev(COL_HIN), halo_next(COL_GC), halo_next(COL_HIN),
                  pl.BlockSpec((None, tm, N_HEADS * DV), lambda b, i: (b, i, 0)),
                  pl.BlockSpec((None, tm, N_HEADS * DV), lambda b, i: (b, i, 0)),
                  col(N_HEADS * DV, COL_O),
                  pl.BlockSpec((None, tm, D), lambda b, i: (b, i, 0)),
                  pl.BlockSpec((None, 1, D), lambda b, i: (b, 0, 0)),
                  full((CONV_K, CONV_W)), full((1, CONV_W)), full((1, N_HEADS * DV)),
                  full((D, D)), full((1, D)), full((1, D))],
        out_specs=pl.BlockSpec((None, tm, D), lambda b, i: (b, i, 0)),
        out_shape=jax.ShapeDtypeStruct((bsz, s, D), F32),
        scratch_shapes=[pltpu.VMEM((tm, D), BF16)],
        compiler_params=_cparams(("arbitrary", "arbitrary")),
        name="mixout",
    )(proj, proj, proj, proj, proj, proj, proj, hf, hb, proj, x, g1,
      conv_w, conv_b, norm_w, w_out, ln_w, ln_b)


def _peerq_kernel(x1_ref, sh_ref, sc_ref, wq_ref, keys_ref, ut_ref, st_ref):
    u = _ln(x1_ref[...]) * (1.0 + sc_ref[...]) + sh_ref[...]
    ut_ref[...] = u.T.astype(BF16)
    qh = _bdot(u.astype(BF16), wq_ref[...]).astype(BF16)
    for hp in range(2 * P_HEADS):
        st_ref[hp] = lax.dot_general(keys_ref[hp], qh[:, hp * N_KEYS:(hp + 1) * N_KEYS],
                                     (((1,), (1,)), ((), ())), preferred_element_type=F32)


def _peerq(x1, sh, sc, wq, keys):
    bsz, s, _ = x1.shape
    tm = min(PQ_TM, s)
    nt = s // tm
    t = bsz * s
    return pl.pallas_call(
        _peerq_kernel,
        grid=(bsz, nt),
        in_specs=[pl.BlockSpec((None, tm, D), lambda b, i: (b, i, 0)),
                  pl.BlockSpec((None, 1, D), lambda b, i: (b, 0, 0)),
                  pl.BlockSpec((None, 1, D), lambda b, i: (b, 0, 0)),
                  pl.BlockSpec((D, D), lambda b, i: (0, 0)),
                  pl.BlockSpec((2 * P_HEADS, N_KEYS, N_KEYS), lambda b, i: (0, 0, 0))],
        out_specs=[pl.BlockSpec((D, tm), lambda b, i: (0, b * nt + i)),
                   pl.BlockSpec((2 * P_HEADS, N_KEYS, tm), lambda b, i: (0, 0, b * nt + i))],
        out_shape=[jax.ShapeDtypeStruct((D, t), BF16),
                   jax.ShapeDtypeStruct((2 * P_HEADS, N_KEYS, t), F32)],
        compiler_params=_cparams(("arbitrary", "arbitrary")),
        name="peerq",
    )(x1, sh, sc, wq, keys)


def _sort_network(n):
    pairs = []
    p = 1
    while p < n:
        k = p
        while k >= 1:
            for j in range(k % p, n - k, 2 * k):
                for i in range(min(k, n - j - k)):
                    if (i + j) // (2 * p) == (i + j + k) // (2 * p):
                        pairs.append((i + j, i + j + k))
            k //= 2
        p *= 2
    return pairs


def _exchange(v, i, j):
    if v[j] is None:
        return
    if v[i] is None:
        v[i], v[j] = v[j], None
        return
    v[i], v[j] = jnp.maximum(v[i], v[j]), jnp.minimum(v[i], v[j])


def _top_sorted(v):
    v = list(v)
    for i, j in _sort_network(TOPK):
        _exchange(v, i, j)
    shift = SUBLANE // 2
    while shift >= 1:
        other = [None if x is None else pltpu.roll(x, shift, axis=0) for x in v]
        merged = []
        for r in range(TOPK):
            x, y = v[r], other[TOPK - 1 - r]
            merged.append(y if x is None else x if y is None else jnp.maximum(x, y))
        v = merged
        d = TOPK // 2
        while d >= 1:
            for i in range(TOPK):
                if i & d == 0:
                    _exchange(v, i, i + d)
            d //= 2
        shift //= 2
    return v


def _pack_rows(v):
    sub = lax.broadcasted_iota(jnp.int32, v[0].shape, 0)
    groups = []
    for g in range(0, len(v), SUBLANE):
        x = v[g]
        for r in range(1, SUBLANE):
            x = jnp.where(sub == r, v[g + r], x)
        groups.append(x)
    return groups


def _candidate_sums(a, b):
    b_lo, b_hi = _pack_rows(b)
    a_hi = _pack_rows(a)[1]
    parts = [a[0] + b_lo, a[0] + b_hi]
    parts += [a[i] + b_lo for i in range(1, SUBLANE)]
    parts.append(a_hi + b[0])
    return parts + [None] * (TOPK - len(parts))


def _peersel_kernel(st_ref, t1_ref, t2_ref, tau_ref):
    log2e = 1.4426950408889634

    def head(h, carry):
        s1 = st_ref[2 * h]
        s2 = st_ref[2 * h + 1]
        split = lambda x: [x[r * SUBLANE:(r + 1) * SUBLANE, :] for r in range(N_KEYS // SUBLANE)]
        a = _top_sorted(split(s1))
        b = _top_sorted(split(s2))
        a0, b0 = a[0][0:1, :], b[0][0:1, :]
        a_sh = [x - a0 for x in a]
        b_sh = [x - b0 for x in b]
        best = _top_sorted(_candidate_sums(a_sh, b_sh))
        z = jnp.exp(best[0])
        for x in best[1:]:
            z = z + jnp.exp(x)
        log_z = jnp.log(z[0:1, :])
        t1_ref[h] = ((s1 - a0) - log_z) * log2e
        t2_ref[h] = (s2 - b0) * log2e
        best_t = _top_sorted(_candidate_sums([(x - log_z) * log2e for x in a_sh],
                                             [x * log2e for x in b_sh]))
        tau_ref[pl.ds(h, 1), :] = best_t[TOPK - 1][0:1, :]
        return carry

    lax.fori_loop(0, P_HEADS, head, 0)


def _peersel(st):
    t = st.shape[2]
    tm = SEL_TM
    return pl.pallas_call(
        _peersel_kernel,
        grid=(t // tm,),
        in_specs=[pl.BlockSpec((2 * P_HEADS, N_KEYS, tm), lambda i: (0, 0, i))],
        out_specs=[pl.BlockSpec((P_HEADS, N_KEYS, tm), lambda i: (0, 0, i)),
                   pl.BlockSpec((P_HEADS, N_KEYS, tm), lambda i: (0, 0, i)),
                   pl.BlockSpec((P_HEADS, tm), lambda i: (0, i))],
        out_shape=[jax.ShapeDtypeStruct((P_HEADS, N_KEYS, t), F32),
                   jax.ShapeDtypeStruct((P_HEADS, N_KEYS, t), F32),
                   jax.ShapeDtypeStruct((P_HEADS, t), F32)],
        compiler_params=_cparams(("arbitrary",)),
        name="peersel",
    )(st)


def _gelu(x):
    return 0.5 * x * (1.0 + lax.erf(x * (2.0 ** -0.5)))


def _peer_gate_tiles(block, h_ref, a_ref, t1_ref, t2_ref, tau_ref):
    eb, tm = h_ref.shape
    keys_per_step = eb // N_KEYS
    first_key = block * keys_per_step
    group = pl.multiple_of((first_key // SUBLANE) * SUBLANE, SUBLANE)
    offset = first_key - group

    def tile(kk, part, c):
        keys2 = slice(part * PEER_GATE_ROWS, (part + 1) * PEER_GATE_ROWS)
        rows = slice(kk * N_KEYS + keys2.start, kk * N_KEYS + keys2.stop)
        cols = slice(c * LANE, (c + 1) * LANE)
        w = jnp.zeros((PEER_GATE_ROWS, LANE), F32)
        for h in range(P_HEADS):
            t1_group = t1_ref[h, pl.ds(group, SUBLANE), cols]
            t1_row = t1_group[kk:kk + 1, :]
            for g in range(1, SUBLANE // keys_per_step):
                r = g * keys_per_step + kk
                t1_row = jnp.where(offset == g * keys_per_step, t1_group[r:r + 1, :], t1_row)
            s = t1_row + t2_ref[h, keys2, cols]
            w = w + jnp.where(s >= tau_ref[h:h + 1, cols], jnp.exp2(s), 0.0)
        a_ref[rows, cols] = (_gelu(h_ref[rows, cols]) * w).astype(BF16)

    return [functools.partial(tile, kk, part, c) for kk in range(keys_per_step)
            for part in range(N_KEYS // PEER_GATE_ROWS) for c in range(tm // LANE)]


def _matmul_pieces(out_ref, lhs_ref, rhs_ref, tile_m, tile_n, tile_k, accumulate):
    def piece(m, n, k):
        rows = slice(m * tile_m, (m + 1) * tile_m)
        cols = slice(n * tile_n, (n + 1) * tile_n)
        depth = slice(k * tile_k, (k + 1) * tile_k)
        res = _bdot(lhs_ref[rows, depth], rhs_ref[depth, cols])
        out_ref[rows, cols] = out_ref[rows, cols] + res if (accumulate or k > 0) else res

    return [functools.partial(piece, m, n, k)
            for n in range(out_ref.shape[1] // tile_n) for m in range(out_ref.shape[0] // tile_m)
            for k in range(lhs_ref.shape[1] // tile_k)]


def _peer_kernel(ut_ref, pu_ref, pvt_ref, t1_ref, t2_ref, tau_ref, x1_ref, g2_ref, lw_ref, lb_ref,
                 out_ref, acc_scr, h0_scr, h1_scr, a0_scr, a1_scr):
    s = pl.program_id(1)
    last = pl.num_programs(1) - 1
    n_blocks = last - 1

    @pl.when(s == 0)
    def _():
        acc_scr[...] = jnp.zeros_like(acc_scr)
        a0_scr[...] = jnp.zeros_like(a0_scr)
        h0_scr[...] = _bdot(pu_ref[...], ut_ref[...])

    def stages(h_write, h_read, a_write, a_read):
        block = jnp.minimum(s - 1, n_blocks - 1)
        gate = _peer_gate_tiles(block, h_read, a_write, t1_ref, t2_ref, tau_ref)
        mxu = (_matmul_pieces(h_write, pu_ref, ut_ref, PEER_MM_TILE_M, PEER_MM_TILE_N, PEER_MM_TILE_K1, False)
               + _matmul_pieces(acc_scr, pvt_ref, a_read, PEER_MM_TILE_M, PEER_MM_TILE_N, a_read.shape[0], True))
        issued = 0
        for i, gate_tile in enumerate(gate):
            ahead = min(len(mxu), -(-(i + 1 + PEER_MXU_LEAD) * len(mxu) // len(gate)))
            for piece in mxu[issued:ahead]:
                piece()
            issued = ahead
            gate_tile()

    @pl.when((s % 2 == 0) & (s > 0))
    def _():
        stages(h0_scr, h1_scr, a0_scr, a1_scr)

    @pl.when((s % 2 == 1) & (s < last))
    def _():
        stages(h1_scr, h0_scr, a1_scr, a0_scr)

    @pl.when(s == last)
    def _():
        acc_scr[...] += _bdot(pvt_ref[...], a0_scr[...])
        y = acc_scr[...].T
        out_ref[...] = _ln(ALPHA * x1_ref[...] + g2_ref[...] * y) * lw_ref[...] + lb_ref[...]


def _peer(ut, pu, pvt, t1, t2, tau, x1, g2, ln_w, ln_b):
    bsz, s, _ = x1.shape
    tm = min(PEER_TM, s)
    nt = s // tm
    eb = PEER_EB
    nb = N_EXPERTS // eb
    assert nb % 2 == 0
    return pl.pallas_call(
        _peer_kernel,
        grid=(bsz * nt, nb + 2),
        in_specs=[pl.BlockSpec((D, tm), lambda i, j: (0, i)),
                  pl.BlockSpec((eb, D), lambda i, j: (jnp.minimum(j, nb - 1), 0)),
                  pl.BlockSpec((None, D, eb), lambda i, j: (jnp.clip(j - 2, 0, nb - 1), 0, 0)),
                  pl.BlockSpec((P_HEADS, N_KEYS, tm), lambda i, j: (0, 0, i)),
                  pl.BlockSpec((P_HEADS, N_KEYS, tm), lambda i, j: (0, 0, i)),
                  pl.BlockSpec((P_HEADS, tm), lambda i, j: (0, i)),
                  pl.BlockSpec((None, tm, D), lambda i, j: (i // nt, i % nt, 0)),
                  pl.BlockSpec((None, 1, D), lambda i, j: (i // nt, 0, 0)),
                  pl.BlockSpec((1, D), lambda i, j: (0, 0)),
                  pl.BlockSpec((1, D), lambda i, j: (0, 0))],
        out_specs=pl.BlockSpec((None, tm, D), lambda i, j: (i // nt, i % nt, 0)),
        out_shape=jax.ShapeDtypeStruct((bsz, s, D), F32),
        scratch_shapes=[pltpu.VMEM((D, tm), F32),
                        pltpu.VMEM((eb, tm), F32), pltpu.VMEM((eb, tm), F32),
                        pltpu.VMEM((eb, tm), BF16), pltpu.VMEM((eb, tm), BF16)],
        compiler_params=_cparams(("arbitrary", "arbitrary")),
        name="peer",
    )(ut, pu, pvt, t1, t2, tau, x1, g2, ln_w, ln_b)


def _layer(x, mod, w_in, b_in, conv_w, conv_b, norm_w, w_out, ln1_w, ln1_b,
           wq, subkeys, pu, pv, ln2_w, ln2_b):
    bsz = x.shape[0]
    sh1, sc1, g1, sh2, sc2, g2 = [m.reshape(bsz, 1, D) for m in jnp.split(mod, 6, axis=-1)]
    row = lambda v: v.reshape(1, -1)

    w_main = w_in[:, :MAIN_COLS].astype(BF16)
    w_gate = jnp.pad(w_in[:, MAIN_COLS:], ((0, 0), (0, LANE - N_GATE))).astype(BF16)
    b_gate = jnp.pad(b_in[MAIN_COLS:], (0, LANE - N_GATE))
    proj, gates = _inproj(x, sh1, sc1, w_main, row(b_in[:MAIN_COLS]), w_gate, row(b_gate))
    gates_t = jnp.swapaxes(gates[:, :, :N_GATE], 1, 2)
    hf, hb = _mlstm(proj, gates, gates_t)
    x1 = _mixout(proj, hf, hb, x, g1, conv_w, row(conv_b), row(norm_w), w_out.astype(BF16),
                 row(ln1_w), row(ln1_b))
    keys = subkeys.reshape(2 * P_HEADS, N_KEYS, N_KEYS).astype(BF16)
    ut, st = _peerq(x1, sh2, sc2, wq.astype(BF16), keys)
    t1, t2, tau = _peersel(st)
    pvt = jnp.swapaxes(pv.reshape(-1, PEER_EB, D), 1, 2).astype(BF16)
    return _peer(ut, pu.astype(BF16), pvt, t1, t2, tau, x1, g2, row(ln2_w), row(ln2_b))


def kernel(x, c, w_ada, b_ada, w_in, b_in, conv_w, conv_b, mlstm_norm_w, w_out, ln1_w, ln1_b,
           peer_wq, peer_subkeys, peer_u, peer_v, ln2_w, ln2_b):
    for l in range(w_ada.shape[0]):
        mod = _ada(c, w_ada[l], b_ada[l])
        x = _layer(x, mod, w_in[l], b_in[l], conv_w[l], conv_b[l], mlstm_norm_w[l], w_out[l],
                   ln1_w[l], ln1_b[l], peer_wq[l], peer_subkeys[l], peer_u[l], peer_v[l],
                   ln2_w[l], ln2_b[l])
    return x
```

```python
import functools

import jax
import jax.numpy as jnp
from jax import lax
from jax.experimental import pallas as pl
from jax.experimental.pallas import tpu as pltpu

D = 2048
CONV_W = 1024
CONV_K = 3
N_HEADS = 4
DQK = 128
DV = 256
N_GATE = 16
MAIN_COLS = 6144
COL_GB, COL_GC, COL_HIN, COL_Q, COL_K, COL_V, COL_O = 0, 1024, 2048, 3072, 3584, 4096, 5120
P_HEADS = 8
N_KEYS = 128
N_EXPERTS = N_KEYS * N_KEYS
TOPK = 16
LN_EPS = 1e-5
DEPTH = 1
ALPHA = (2 * DEPTH) ** 0.25

LANE = 128
SUBLANE = 8
VMEM_LIMIT = 52 * 1024 * 1024

ADA_TN = 1024
INPROJ_TM, INPROJ_TN = 1024, 1024
MLSTM_L = 256
MIX_TM = 256
PQ_TM = 256
SEL_TM = 128
PEER_TM, PEER_EB = 512, 1024
LN2_TM = 512
MXU_WIDTH = 256
PEER_MM_TILE_N = MXU_WIDTH
PEER_MM_TILE_M = 512
PEER_MM_TILE_K1 = MXU_WIDTH
PEER_MM_TILE_K2 = 512
PEER_GATE_ROWS = 64
PEER_MXU_LEAD = 1

F32 = jnp.float32
BF16 = jnp.bfloat16
NEG_INF = float("-inf")


def _cparams(sem, flags=None):
    return pltpu.CompilerParams(dimension_semantics=sem, vmem_limit_bytes=VMEM_LIMIT, flags=flags)


def _ln(x):
    mu = jnp.mean(x, axis=-1, keepdims=True)
    xc = x - mu
    var = jnp.mean(xc * xc, axis=-1, keepdims=True)
    return xc * lax.rsqrt(var + LN_EPS)


def _bdot(a, b):
    return jnp.dot(a, b, preferred_element_type=F32)


def _ada_kernel(c_ref, w_ref, b_ref, o_ref):
    c = c_ref[...]
    cond = c * jax.nn.sigmoid(c)
    o_ref[...] = _bdot(cond.astype(BF16), w_ref[...].astype(BF16)) + b_ref[...]


def _ada(c, w, b):
    bsz = c.shape[0]
    n = w.shape[1]
    return pl.pallas_call(
        _ada_kernel,
        grid=(n // ADA_TN,),
        in_specs=[pl.BlockSpec((bsz, D), lambda j: (0, 0)),
                  pl.BlockSpec((D, ADA_TN), lambda j: (0, j)),
                  pl.BlockSpec((1, ADA_TN), lambda j: (0, j))],
        out_specs=pl.BlockSpec((bsz, ADA_TN), lambda j: (0, j)),
        out_shape=jax.ShapeDtypeStruct((bsz, n), F32),
        compiler_params=_cparams(("arbitrary",)),
        name="ada",
    )(c, w, b.reshape(1, n))


def _inproj_kernel(x_ref, sh_ref, sc_ref, w_ref, b_ref, wg_ref, bg_ref, proj_ref, gate_ref, u_scr):
    @pl.when(pl.program_id(2) == 0)
    def _():
        u = _ln(x_ref[...]) * (1.0 + sc_ref[...]) + sh_ref[...]
        ub = u.astype(BF16)
        u_scr[...] = ub
        gate_ref[...] = _bdot(ub, wg_ref[...]) + bg_ref[...]

    proj_ref[...] = _bdot(u_scr[...], w_ref[...]) + b_ref[...]


def _inproj(x, sh, sc, w_main, b_main, w_gate, b_gate):
    bsz, s, _ = x.shape
    tm, tn = min(INPROJ_TM, s), INPROJ_TN
    return pl.pallas_call(
        _inproj_kernel,
        grid=(bsz, s // tm, MAIN_COLS // tn),
        in_specs=[pl.BlockSpec((None, tm, D), lambda b, i, j: (b, i, 0)),
                  pl.BlockSpec((None, 1, D), lambda b, i, j: (b, 0, 0)),
                  pl.BlockSpec((None, 1, D), lambda b, i, j: (b, 0, 0)),
                  pl.BlockSpec((D, tn), lambda b, i, j: (0, j)),
                  pl.BlockSpec((1, tn), lambda b, i, j: (0, j)),
                  pl.BlockSpec((D, LANE), lambda b, i, j: (0, 0)),
                  pl.BlockSpec((1, LANE), lambda b, i, j: (0, 0))],
        out_specs=[pl.BlockSpec((None, tm, tn), lambda b, i, j: (b, i, j)),
                   pl.BlockSpec((None, tm, LANE), lambda b, i, j: (b, i, 0))],
        out_shape=[jax.ShapeDtypeStruct((bsz, s, MAIN_COLS), F32),
                   jax.ShapeDtypeStruct((bsz, s, LANE), F32)],
        scratch_shapes=[pltpu.VMEM((tm, D), BF16)],
        compiler_params=_cparams(("arbitrary", "arbitrary", "arbitrary")),
        name="inproj",
    )(x, sh, sc, w_main, b_main, w_gate, b_gate)


def _log_sigmoid(x):
    return jnp.minimum(x, 0.0) - jnp.log1p(jnp.exp(-jnp.abs(x)))


def _mlstm_chains(d, q_ref, k_ref, v_ref, gcol_ref, grow_ref, h_ref, c_scr, n_scr, m_scr):
    L = q_ref.shape[0]
    t_idx = lax.broadcasted_iota(jnp.int32, (L, L), 0)
    s_idx = lax.broadcasted_iota(jnp.int32, (L, L), 1)
    mask = (t_idx >= s_idx) if d == 0 else (t_idx <= s_idx)
    mask_f = mask.astype(F32)
    mask_tf = ((t_idx <= s_idx) if d == 0 else (t_idx >= s_idx)).astype(F32)
    gcol = gcol_ref[...]
    grow = grow_ref[...]
    hi = lax.Precision.HIGHEST
    bcols = jnp.dot(mask_f, _log_sigmoid(gcol), precision=hi, preferred_element_type=F32)
    brows = jnp.dot(_log_sigmoid(grow), mask_tf, precision=hi, preferred_element_type=F32)
    last = L - 1 if d == 0 else 0

    def chain(h):
        idx = d * N_HEADS + h
        ci, cf = d * 8 + h, d * 8 + 4 + h
        t = {}

        def weights():
            t["b_col"] = b_col = bcols[:, cf:cf + 1]
            b_row = brows[cf:cf + 1, :]
            ig_row = grow[ci:ci + 1, :]
            t["b_tot"] = brows[cf:cf + 1, last:last + 1]
            t["m_prev"] = m_prev = m_scr[idx][:, 0:1]
            dm = jnp.where(mask, b_col - b_row + ig_row, NEG_INF)
            inter = b_col + m_prev
            t["m_t"] = m_t = jnp.maximum(inter, jnp.max(dm, axis=1, keepdims=True))
            t["w_intra"] = jnp.exp(dm - m_t)
            t["w_inter"] = jnp.exp(inter - m_t)

        def scores():
            t["q"] = q = q_ref[:, h * DQK:(h + 1) * DQK] * (DQK ** -0.5)
            t["qb"] = qb = q.astype(BF16)
            t["kb"] = kb = k_ref[:, h * DQK:(h + 1) * DQK].astype(BF16)
            qk = lax.dot_general(qb, kb, (((1,), (1,)), ((), ())), preferred_element_type=F32)
            t["sw"] = qk * t.pop("w_intra")

        def output():
            sw, w_inter, q = t.pop("sw"), t.pop("w_inter"), t.pop("q")
            v = v_ref[:, h * DV:(h + 1) * DV]
            num = (_bdot(sw.astype(BF16), v.astype(BF16))
                   + w_inter * _bdot(t.pop("qb"), c_scr[idx].astype(BF16)))
            den = (jnp.sum(sw, axis=1, keepdims=True)
                   + w_inter * jnp.sum(q * n_scr[idx], axis=1, keepdims=True))
            h_ref[:, h * DV:(h + 1) * DV] = num / jnp.maximum(jnp.abs(den), jnp.exp(-t.pop("m_t")))

        def state():
            b_tot, m_prev = t.pop("b_tot"), t.pop("m_prev")
            a_col = b_tot - t.pop("b_col") + gcol[:, ci:ci + 1]
            m_new = jnp.maximum(b_tot + m_prev, jnp.max(a_col, axis=0, keepdims=True))
            w_a = jnp.exp(a_col - m_new)
            decay = jnp.exp(b_tot + m_prev - m_new)
            k = k_ref[:, h * DQK:(h + 1) * DQK]
            v = v_ref[:, h * DV:(h + 1) * DV]
            kv = lax.dot_general(t.pop("kb"), (w_a * v).astype(BF16), (((0,), (0,)), ((), ())),
                                 preferred_element_type=F32)
            c_scr[idx] = decay * c_scr[idx] + kv
            n_scr[idx] = decay * n_scr[idx] + jnp.sum(w_a * k, axis=0, keepdims=True)
            m_scr[idx] = jnp.broadcast_to(m_new, (1, LANE))

        return [weights, scores, output, state]

    return [chain(h) for h in range(N_HEADS)]


def _mlstm_kernel(qf, kf, vf, gcf, grf, qb, kb, vb, gcb, grb, hf_ref, hb_ref, c_scr, n_scr, m_scr):
    @pl.when(pl.program_id(1) == 0)
    def _():
        c_scr[...] = jnp.zeros_like(c_scr)
        n_scr[...] = jnp.zeros_like(n_scr)
        m_scr[...] = jnp.zeros_like(m_scr)

    chains = (_mlstm_chains(0, qf, kf, vf, gcf, grf, hf_ref, c_scr, n_scr, m_scr)
              + _mlstm_chains(1, qb, kb, vb, gcb, grb, hb_ref, c_scr, n_scr, m_scr))
    for chain in chains:
        for step in chain:
            step()


def _mlstm(proj, gates, gates_t):
    bsz, s, _ = proj.shape
    L = min(MLSTM_L, s)
    nc = s // L
    qkw, vw = N_HEADS * DQK, N_HEADS * DV

    def specs(row):
        return [pl.BlockSpec((None, L, qkw), lambda b, j: (b, row(j), COL_Q // qkw)),
                pl.BlockSpec((None, L, qkw), lambda b, j: (b, row(j), COL_K // qkw)),
                pl.BlockSpec((None, L, vw), lambda b, j: (b, row(j), COL_V // vw)),
                pl.BlockSpec((None, L, LANE), lambda b, j: (b, row(j), 0)),
                pl.BlockSpec((None, N_GATE, L), lambda b, j: (b, 0, row(j)))]

    fwd = lambda j: j
    bwd = lambda j: nc - 1 - j
    return pl.pallas_call(
        _mlstm_kernel,
        grid=(bsz, nc),
        in_specs=specs(fwd) + specs(bwd),
        out_specs=[pl.BlockSpec((None, L, vw), lambda b, j: (b, j, 0)),
                   pl.BlockSpec((None, L, vw), lambda b, j: (b, nc - 1 - j, 0))],
        out_shape=[jax.ShapeDtypeStruct((bsz, s, vw), F32)] * 2,
        scratch_shapes=[pltpu.VMEM((2 * N_HEADS, DQK, DV), F32),
                        pltpu.VMEM((2 * N_HEADS, 1, DQK), F32),
                        pltpu.VMEM((2 * N_HEADS, 1, LANE), F32)],
        compiler_params=_cparams(("arbitrary", "arbitrary")),
        name="mlstm",
    )(proj, proj, proj, gates, gates_t, proj, proj, proj, gates, gates_t)


def _mixout_kernel(gb_ref, gc_ref, hin_ref, gcp_ref, hinp_ref, gcn_ref, hinn_ref, hf_ref, hb_ref, o_ref,
                   x_ref, g1_ref, cw_ref, cb_ref, nw_ref, wout_ref, lw_ref, lb_ref, x1_ref, y_scr):
    i = pl.program_id(1)
    tm = x_ref.shape[0]
    z = gc_ref[...] * hin_ref[...]
    z_before = (gcp_ref[...] * hinp_ref[...])[SUBLANE - 1:SUBLANE, :]
    z_after = (gcn_ref[...] * hinn_ref[...])[0:1, :]
    z_before = jnp.where(i == 0, 0.0, z_before)
    z_after = jnp.where(i == pl.num_programs(1) - 1, 0.0, z_after)
    row = lax.broadcasted_iota(jnp.int32, z.shape, 0)
    z_prev = jnp.where(row == 0, z_before, pltpu.roll(z, 1, axis=0))
    z_next = jnp.where(row == tm - 1, z_after, pltpu.roll(z, tm - 1, axis=0))
    cw = cw_ref[...]
    conv = cw[0:1, :] * z_prev + cw[1:2, :] * z + cw[2:3, :] * z_next + cb_ref[...]
    y_scr[:, 0:CONV_W] = (gb_ref[...] * conv).astype(BF16)
    for h in range(N_HEADS):
        sl = slice(h * DV, (h + 1) * DV)
        hn = _ln(hf_ref[:, sl] + hb_ref[:, sl]) * nw_ref[:, sl]
        y_scr[:, CONV_W + h * DV:CONV_W + (h + 1) * DV] = (jax.nn.sigmoid(o_ref[:, sl]) * hn).astype(BF16)
    y = _bdot(y_scr[...], wout_ref[...])
    x1_ref[...] = _ln(ALPHA * x_ref[...] + g1_ref[...] * y) * lw_ref[...] + lb_ref[...]


def _mixout(proj, hf, hb, x, g1, conv_w, conv_b, norm_w, w_out, ln_w, ln_b):
    bsz, s, _ = x.shape
    tm = min(MIX_TM, s)
    nt = s // tm
    rb = tm // SUBLANE
    last_rb = s // SUBLANE - 1

    def col(width, start):
        return pl.BlockSpec((None, tm, width), lambda b, i: (b, i, start // width))

    def halo_prev(start):
        return pl.BlockSpec((None, SUBLANE, CONV_W),
                            lambda b, i: (b, jnp.maximum(i * rb - 1, 0), start // CONV_W))

    def halo_next(start):
        return pl.BlockSpec((None, SUBLANE, CONV_W),
                            lambda b, i: (b, jnp.minimum((i + 1) * rb, last_rb), start // CONV_W))

    full = lambda shape: pl.BlockSpec(shape, lambda b, i: tuple(0 for _ in shape))
    return pl.pallas_call(
        _mixout_kernel,
        grid=(bsz, nt),
        in_specs=[col(CONV_W, COL_GB), col(CONV_W, COL_GC), col(CONV_W, COL_HIN),
                  halo_prev(COL_GC), halo_prev(COL_HIN), halo_next(COL_GC), halo_next(COL_HIN),
                  pl.BlockSpec((None, tm, N_HEADS * DV), lambda b, i: (b, i, 0)),
                  pl.BlockSpec((None, tm, N_HEADS * DV), lambda b, i: (b, i, 0)),
                  col(N_HEADS * DV, COL_O),
                  pl.BlockSpec((None, tm, D), lambda b, i: (b, i, 0)),
                  pl.BlockSpec((None, 1, D), lambda b, i: (b, 0, 0)),
                  full((CONV_K, CONV_W)), full((1, CONV_W)), full((1, N_HEADS * DV)),
                  full((D, D)), full((1, D)), full((1, D))],
        out_specs=pl.BlockSpec((None, tm, D), lambda b, i: (b, i, 0)),
        out_shape=jax.ShapeDtypeStruct((bsz, s, D), F32),
        scratch_shapes=[pltpu.VMEM((tm, D), BF16)],
        compiler_params=_cparams(("arbitrary", "arbitrary")),
        name="mixout",
    )(proj, proj, proj, proj, proj, proj, proj, hf, hb, proj, x, g1,
      conv_w, conv_b, norm_w, w_out, ln_w, ln_b)


def _peerq_kernel(x1_ref, sh_ref, sc_ref, wq_ref, keys_ref, ut_ref, st_ref):
    u = _ln(x1_ref[...]) * (1.0 + sc_ref[...]) + sh_ref[...]
    ut_ref[...] = u.T.astype(BF16)
    qh = _bdot(u.astype(BF16), wq_ref[...]).astype(BF16)
    for hp in range(2 * P_HEADS):
        st_ref[hp] = lax.dot_general(keys_ref[hp], qh[:, hp * N_KEYS:(hp + 1) * N_KEYS],
                                     (((1,), (1,)), ((), ())), preferred_element_type=F32)


def _peerq(x1, sh, sc, wq, keys):
    bsz, s, _ = x1.shape
    tm = min(PQ_TM, s)
    nt = s // tm
    t = bsz * s
    return pl.pallas_call(
        _peerq_kernel,
        grid=(bsz, nt),
        in_specs=[pl.BlockSpec((None, tm, D), lambda b, i: (b, i, 0)),
                  pl.BlockSpec((None, 1, D), lambda b, i: (b, 0, 0)),
                  pl.BlockSpec((None, 1, D), lambda b, i: (b, 0, 0)),
                  pl.BlockSpec((D, D), lambda b, i: (0, 0)),
                  pl.BlockSpec((2 * P_HEADS, N_KEYS, N_KEYS), lambda b, i: (0, 0, 0))],
        out_specs=[pl.BlockSpec((D, tm), lambda b, i: (0, b * nt + i)),
                   pl.BlockSpec((2 * P_HEADS, N_KEYS, tm), lambda b, i: (0, 0, b * nt + i))],
        out_shape=[jax.ShapeDtypeStruct((D, t), BF16),
                   jax.ShapeDtypeStruct((2 * P_HEADS, N_KEYS, t), F32)],
        compiler_params=_cparams(("arbitrary", "arbitrary")),
        name="peerq",
    )(x1, sh, sc, wq, keys)


def _sort_network(n):
    pairs = []
    p = 1
    while p < n:
        k = p
        while k >= 1:
            for j in range(k % p, n - k, 2 * k):
                for i in range(min(k, n - j - k)):
                    if (i + j) // (2 * p) == (i + j + k) // (2 * p):
                        pairs.append((i + j, i + j + k))
            k //= 2
        p *= 2
    return pairs


def _exchange(v, i, j):
    if v[j] is None:
        return
    if v[i] is None:
        v[i], v[j] = v[j], None
        return
    v[i], v[j] = jnp.maximum(v[i], v[j]), jnp.minimum(v[i], v[j])


def _top_sorted(v):
    v = list(v)
    for i, j in _sort_network(TOPK):
        _exchange(v, i, j)
    shift = SUBLANE // 2
    while shift >= 1:
        other = [None if x is None else pltpu.roll(x, shift, axis=0) for x in v]
        merged = []
        for r in range(TOPK):
            x, y = v[r], other[TOPK - 1 - r]
            merged.append(y if x is None else x if y is None else jnp.maximum(x, y))
        v = merged
        d = TOPK // 2
        while d >= 1:
            for i in range(TOPK):
                if i & d == 0:
                    _exchange(v, i, i + d)
            d //= 2
        shift //= 2
    return v


def _pack_rows(v):
    sub = lax.broadcasted_iota(jnp.int32, v[0].shape, 0)
    groups = []
    for g in range(0, len(v), SUBLANE):
        x = v[g]
        for r in range(1, SUBLANE):
            x = jnp.where(sub == r, v[g + r], x)
        groups.append(x)
    return groups


def _candidate_sums(a, b):
    b_lo, b_hi = _pack_rows(b)
    a_hi = _pack_rows(a)[1]
    parts = [a[0] + b_lo, a[0] + b_hi]
    parts += [a[i] + b_lo for i in range(1, SUBLANE)]
    parts.append(a_hi + b[0])
    return parts + [None] * (TOPK - len(parts))


def _peersel_kernel(st_ref, t1_ref, t2_ref, tau_ref):
    log2e = 1.4426950408889634

    def head(h, carry):
        s1 = st_ref[2 * h]
        s2 = st_ref[2 * h + 1]
        split = lambda x: [x[r * SUBLANE:(r + 1) * SUBLANE, :] for r in range(N_KEYS // SUBLANE)]
        a = _top_sorted(split(s1))
        b = _top_sorted(split(s2))
        a0, b0 = a[0][0:1, :], b[0][0:1, :]
        a_sh = [x - a0 for x in a]
        b_sh = [x - b0 for x in b]
        best = _top_sorted(_candidate_sums(a_sh, b_sh))
        z = jnp.exp(best[0])
        for x in best[1:]:
            z = z + jnp.exp(x)
        log_z = jnp.log(z[0:1, :])
        t1_ref[h] = ((s1 - a0) - log_z) * log2e
        t2_ref[h] = (s2 - b0) * log2e
        best_t = _top_sorted(_candidate_sums([(x - log_z) * log2e for x in a_sh],
                                             [x * log2e for x in b_sh]))
        tau_ref[pl.ds(h, 1), :] = best_t[TOPK - 1][0:1, :]
        return carry

    lax.fori_loop(0, P_HEADS, head, 0)


def _peersel(st):
    t = st.shape[2]
    tm = SEL_TM
    return pl.pallas_call(
        _peersel_kernel,
        grid=(t // tm,),
        in_specs=[pl.BlockSpec((2 * P_HEADS, N_KEYS, tm), lambda i: (0, 0, i))],
        out_specs=[pl.BlockSpec((P_HEADS, N_KEYS, tm), lambda i: (0, 0, i)),
                   pl.BlockSpec((P_HEADS, N_KEYS, tm), lambda i: (0, 0, i)),
                   pl.BlockSpec((P_HEADS, tm), lambda i: (0, i))],
        out_shape=[jax.ShapeDtypeStruct((P_HEADS, N_KEYS, t), F32),
                   jax.ShapeDtypeStruct((P_HEADS, N_KEYS, t), F32),
                   jax.ShapeDtypeStruct((P_HEADS, t), F32)],
        compiler_params=_cparams(("arbitrary",)),
        name="peersel",
    )(st)


def _gelu(x):
    return 0.5 * x * (1.0 + lax.erf(x * (2.0 ** -0.5)))


def _peer_gate_tiles(block, h_ref, a_ref, t1_ref, t2_ref, tau_ref):
    eb, tm = h_ref.shape
    keys_per_step = eb // N_KEYS
    first_key = block * keys_per_step
    group = pl.multiple_of((first_key // SUBLANE) * SUBLANE, SUBLANE)
    offset = first_key - group

    def tile(kk, part, c):
        keys2 = slice(part * PEER_GATE_ROWS, (part + 1) * PEER_GATE_ROWS)
        rows = slice(kk * N_KEYS + keys2.start, kk * N_KEYS + keys2.stop)
        cols = slice(c * LANE, (c + 1) * LANE)
        w = jnp.zeros((PEER_GATE_ROWS, LANE), F32)
        for h in range(P_HEADS):
            t1_group = t1_ref[h, pl.ds(group, SUBLANE), cols]
            t1_row = t1_group[kk:kk + 1, :]
            for g in range(1, SUBLANE // keys_per_step):
                r = g * keys_per_step + kk
                t1_row = jnp.where(offset == g * keys_per_step, t1_group[r:r + 1, :], t1_row)
            s = t1_row + t2_ref[h, keys2, cols]
            w = w + jnp.where(s >= tau_ref[h:h + 1, cols], jnp.exp2(s), 0.0)
        a_ref[rows, cols] = (_gelu(h_ref[rows, cols]) * w).astype(BF16)

    return [functools.partial(tile, kk, part, c) for kk in range(keys_per_step)
            for part in range(N_KEYS // PEER_GATE_ROWS) for c in range(tm // LANE)]


def _matmul_pieces(out_ref, lhs_ref, rhs_ref, tile_m, tile_n, tile_k, accumulate):
    def piece(m, n, k):
        rows = slice(m * tile_m, (m + 1) * tile_m)
        cols = slice(n * tile_n, (n + 1) * tile_n)
        depth = slice(k * tile_k, (k + 1) * tile_k)
        res = _bdot(lhs_ref[rows, depth], rhs_ref[depth, cols])
        out_ref[rows, cols] = out_ref[rows, cols] + res if (accumulate or k > 0) else res

    return [functools.partial(piece, m, n, k)
            for n in range(out_ref.shape[1] // tile_n) for m in range(out_ref.shape[0] // tile_m)
            for k in range(lhs_ref.shape[1] // tile_k)]


def _peer_kernel(ut_ref, pu_ref, pvt_ref, t1_ref, t2_ref, tau_ref,
                 out_ref, acc_scr, h0_scr, h1_scr, a0_scr, a1_scr):
    s = pl.program_id(1)
    last = pl.num_programs(1) - 1
    n_blocks = last - 1

    @pl.when(s == 0)
    def _():
        acc_scr[...] = jnp.zeros_like(acc_scr)
        a0_scr[...] = jnp.zeros_like(a0_scr)
        h0_scr[...] = _bdot(pu_ref[...], ut_ref[...])

    def stages(h_write, h_read, a_write, a_read):
        block = jnp.minimum(s - 1, n_blocks - 1)
        gate = _peer_gate_tiles(block, h_read, a_write, t1_ref, t2_ref, tau_ref)
        mxu = (_matmul_pieces(h_write, pu_ref, ut_ref, PEER_MM_TILE_M, PEER_MM_TILE_N, PEER_MM_TILE_K1, False)
               + _matmul_pieces(acc_scr, pvt_ref, a_read, PEER_MM_TILE_M, PEER_MM_TILE_N, PEER_MM_TILE_K2, True))
        issued = 0
        for i, gate_tile in enumerate(gate):
            ahead = min(len(mxu), -(-(i + 1 + PEER_MXU_LEAD) * len(mxu) // len(gate)))
            for piece in mxu[issued:ahead]:
                piece()
            issued = ahead
            gate_tile()

    @pl.when((s % 2 == 0) & (s > 0))
    def _():
        stages(h0_scr, h1_scr, a0_scr, a1_scr)

    @pl.when((s % 2 == 1) & (s < last))
    def _():
        stages(h1_scr, h0_scr, a1_scr, a0_scr)

    @pl.when(s == last)
    def _():
        acc_scr[...] += _bdot(pvt_ref[...], a0_scr[...])
        out_ref[...] = acc_scr[...].T


def _ln2_kernel(x1_ref, y_ref, g2_ref, lw_ref, lb_ref, out_ref):
    out_ref[...] = _ln(ALPHA * x1_ref[...] + g2_ref[...] * y_ref[...]) * lw_ref[...] + lb_ref[...]


def _ln2(x1, y, g2, ln_w, ln_b):
    bsz, s, _ = x1.shape
    tm = min(LN2_TM, s)
    tile = pl.BlockSpec((None, tm, D), lambda b, i: (b, i, 0))
    return pl.pallas_call(
        _ln2_kernel,
        grid=(bsz, s // tm),
        in_specs=[tile, tile, pl.BlockSpec((None, 1, D), lambda b, i: (b, 0, 0)),
                  pl.BlockSpec((1, D), lambda b, i: (0, 0)), pl.BlockSpec((1, D), lambda b, i: (0, 0))],
        out_specs=tile,
        out_shape=jax.ShapeDtypeStruct(x1.shape, F32),
        compiler_params=_cparams(("arbitrary", "arbitrary")),
        name="ln2",
    )(x1, y, g2, ln_w, ln_b)


def _peer(ut, pu, pvt, t1, t2, tau, bsz, s):
    tm = min(PEER_TM, s)
    nt = s // tm
    eb = PEER_EB
    nb = N_EXPERTS // eb
    assert nb % 2 == 0
    return pl.pallas_call(
        _peer_kernel,
        grid=(bsz * nt, nb + 2),
        in_specs=[pl.BlockSpec((D, tm), lambda i, j: (0, i)),
                  pl.BlockSpec((eb, D), lambda i, j: (jnp.minimum(j, nb - 1), 0)),
                  pl.BlockSpec((None, D, eb), lambda i, j: (jnp.clip(j - 2, 0, nb - 1), 0, 0)),
                  pl.BlockSpec((P_HEADS, N_KEYS, tm), lambda i, j: (0, 0, i)),
                  pl.BlockSpec((P_HEADS, N_KEYS, tm), lambda i, j: (0, 0, i)),
                  pl.BlockSpec((P_HEADS, tm), lambda i, j: (0, i))],
        out_specs=pl.BlockSpec((None, tm, D), lambda i, j: (i // nt, i % nt, 0)),
        out_shape=jax.ShapeDtypeStruct((bsz, s, D), F32),
        scratch_shapes=[pltpu.VMEM((D, tm), F32),
                        pltpu.VMEM((eb, tm), F32), pltpu.VMEM((eb, tm), F32),
                        pltpu.VMEM((eb, tm), BF16), pltpu.VMEM((eb, tm), BF16)],
        compiler_params=_cparams(("arbitrary", "arbitrary")),
        name="peer",
    )(ut, pu, pvt, t1, t2, tau)


def _layer(x, mod, w_in, b_in, conv_w, conv_b, norm_w, w_out, ln1_w, ln1_b,
           wq, subkeys, pu, pv, ln2_w, ln2_b):
    bsz = x.shape[0]
    sh1, sc1, g1, sh2, sc2, g2 = [m.reshape(bsz, 1, D) for m in jnp.split(mod, 6, axis=-1)]
    row = lambda v: v.reshape(1, -1)

    w_main = w_in[:, :MAIN_COLS].astype(BF16)
    w_gate = jnp.pad(w_in[:, MAIN_COLS:], ((0, 0), (0, LANE - N_GATE))).astype(BF16)
    b_gate = jnp.pad(b_in[MAIN_COLS:], (0, LANE - N_GATE))
    proj, gates = _inproj(x, sh1, sc1, w_main, row(b_in[:MAIN_COLS]), w_gate, row(b_gate))
    gates_t = jnp.swapaxes(gates[:, :, :N_GATE], 1, 2)
    hf, hb = _mlstm(proj, gates, gates_t)
    x1 = _mixout(proj, hf, hb, x, g1, conv_w, row(conv_b), row(norm_w), w_out.astype(BF16),
                 row(ln1_w), row(ln1_b))
    keys = subkeys.reshape(2 * P_HEADS, N_KEYS, N_KEYS).astype(BF16)
    ut, st = _peerq(x1, sh2, sc2, wq.astype(BF16), keys)
    t1, t2, tau = _peersel(st)
    pvt = jnp.swapaxes(pv.reshape(-1, PEER_EB, D), 1, 2).astype(BF16)
    y = _peer(ut, pu.astype(BF16), pvt, t1, t2, tau, bsz, x.shape[1])
    return _ln2(x1, y, g2, row(ln2_w), row(ln2_b))


def kernel(x, c, w_ada, b_ada, w_in, b_in, conv_w, conv_b, mlstm_norm_w, w_out, ln1_w, ln1_b,
           peer_wq, peer_subkeys, peer_u, peer_v, ln2_w, ln2_b):
    for l in range(w_ada.shape[0]):
        mod = _ada(c, w_ada[l], b_ada[l])
        x = _layer(x, mod, w_in[l], b_in[l], conv_w[l], conv_b[l], mlstm_norm_w[l], w_out[l],
                   ln1_w[l], ln1_b[l], peer_wq[l], peer_subkeys[l], peer_u[l], peer_v[l],
                   ln2_w[l], ln2_b[l])
    return x
```
